```python
import jax, jax.numpy as jnp
from jax import lax
import numpy as np

D_MODEL = 1024
BATCH = 8
SEQ = 4096
DEPTH = 2

MEM_LEN = 256
POOL_WIDTH = 512
POOL_GROUPS = 4
POOL_WINDOWS = (2, 4, 8, 16)
POOL_GROUP_DIM = POOL_WIDTH // POOL_GROUPS
SGU_WIDTH = 512
SGU_GROUPS = 4
SGU_GROUP_DIM = SGU_WIDTH // SGU_GROUPS
CHUNK = 128
IN_COLS = POOL_WIDTH + 2 * SGU_WIDTH + 2 * D_MODEL
XATTN_HEADS = 4
XATTN_HEAD_DIM = D_MODEL // XATTN_HEADS
D_FF = 2816
N_EXPERTS = 8
TOP_K = 2
D_FF_EXPERT = 3584
N_DENSE = (DEPTH + 1) // 2
N_MOE = DEPTH // 2
EPS = 1e-6

kernel_name = "hybrid_pool_sgu_xattn_moe_trunk"


def rms_norm(x, g):
    xf = x.astype(jnp.float32)
    y = xf * lax.rsqrt(jnp.mean(xf * xf, axis=-1, keepdims=True) + EPS)
    return (y * g.astype(jnp.float32)).astype(x.dtype)


def layer_norm(x, g, b):
    xf = x.astype(jnp.float32)
    mu = jnp.mean(xf, axis=-1, keepdims=True)
    xc = xf - mu
    y = xc * lax.rsqrt(jnp.mean(xc * xc, axis=-1, keepdims=True) + EPS)
    return (y * g.astype(jnp.float32) + b.astype(jnp.float32)).astype(x.dtype)


def pool_mixer(p, w_mix, scale):
    b, s, _ = p.shape
    pf = p.astype(jnp.float32)
    cs = jnp.cumsum(pf, axis=1)
    t = jnp.arange(s)
    outs = []
    for gi, w in enumerate(POOL_WINDOWS):
        sl = slice(gi * POOL_GROUP_DIM, (gi + 1) * POOL_GROUP_DIM)
        c = cs[..., sl]
        lag = jnp.pad(c, ((0, 0), (w, 0), (0, 0)))[:, :s]
        cnt = jnp.minimum(t + 1, w).astype(jnp.float32)[None, :, None]
        outs.append((c - lag) / cnt - pf[..., sl])
    pooled = jnp.stack(outs, axis=2).astype(p.dtype)
    mixed = jnp.einsum("bsgc,gcd->bsgd", pooled, w_mix)
    return mixed.reshape(b, s, POOL_WIDTH) * scale


def spatial_gating(u, v, ln_g, ln_b, w_sp, b_sp):
    b, s, _ = u.shape
    n = s // CHUNK
    v = layer_norm(v, ln_g, ln_b)
    vc = v.reshape(b, n, CHUNK, SGU_GROUPS, SGU_GROUP_DIM)
    mask = jnp.tril(jnp.ones((CHUNK, CHUNK), dtype=bool))
    w = jnp.where(mask, w_sp, jnp.zeros((), w_sp.dtype))
    mixed = jnp.einsum("gts,bnsgc->bntgc", w, vc) + b_sp.T[None, None, :, :, None]
    return u * mixed.reshape(b, s, SGU_WIDTH)


def cross_attention(h, mem_n, w_q, w_kv, w_o):
    b, s, _ = h.shape
    q = (h @ w_q).reshape(b, s, XATTN_HEADS, XATTN_HEAD_DIM)
    kv = (mem_n @ w_kv).reshape(b, MEM_LEN, 2, XATTN_HEADS, XATTN_HEAD_DIM)
    k, v = kv[:, :, 0], kv[:, :, 1]
    sc = jnp.einsum("bshd,bmhd->bhsm", q, k).astype(jnp.float32) * (XATTN_HEAD_DIM ** -0.5)
    pr = jax.nn.softmax(sc, axis=-1).astype(v.dtype)
    o = jnp.einsum("bhsm,bmhd->bshd", pr, v).reshape(b, s, D_MODEL)
    return o @ w_o


def swiglu(h, w_gu, w_down):
    g, up = jnp.split(h @ w_gu, 2, axis=-1)
    return (jax.nn.silu(g) * up) @ w_down


def moe_swiglu(h, w_router, w_gu, w_down):
    logits = (h @ w_router).astype(jnp.float32)
    top_v, top_i = lax.top_k(logits, TOP_K)
    gates = jax.nn.softmax(top_v, axis=-1)
    combine = jnp.sum(jax.nn.one_hot(top_i, N_EXPERTS, dtype=jnp.float32) * gates[..., None], axis=-2)
    combine = combine.astype(h.dtype)
    out = jnp.zeros_like(h)
    for e in range(N_EXPERTS):
        out = out + combine[..., e:e + 1] * swiglu(h, w_gu[e], w_down[e])
    return out


def setup_inputs(seed: int = 0) -> dict:
    key = jax.random.key(seed)
    ks = jax.random.split(key, 32)
    f32 = jnp.float32

    def nrm(k, shape, scale):
        return jax.random.normal(k, shape, f32) * scale

    def gain(k, shape):
        return 1.0 + 0.02 * jax.random.normal(k, shape, f32)

    return {
        "x": nrm(ks[0], (BATCH, SEQ, D_MODEL), 1.0),
        "mem": nrm(ks[1], (BATCH, MEM_LEN, D_MODEL), 1.0),
        "norm_mem": gain(ks[2], (D_MODEL,)),
        "norm_mix": gain(ks[3], (DEPTH, D_MODEL)),
        "w_in": nrm(ks[4], (DEPTH, D_MODEL, IN_COLS), D_MODEL ** -0.5),
        "pool_mix": nrm(ks[5], (DEPTH, POOL_GROUPS, POOL_GROUP_DIM, POOL_GROUP_DIM), POOL_GROUP_DIM ** -0.5),
        "pool_scale": 1.0 + 0.1 * jax.random.normal(ks[6], (DEPTH, POOL_WIDTH), f32),
        "sgu_ln_g": gain(ks[7], (DEPTH, SGU_WIDTH)),
        "sgu_ln_b": nrm(ks[8], (DEPTH, SGU_WIDTH), 0.02),
        "w_spatial": nrm(ks[9], (DEPTH, SGU_GROUPS, CHUNK, CHUNK), CHUNK ** -0.5),
        "b_spatial": 1.0 + 0.1 * jax.random.normal(ks[10], (DEPTH, SGU_GROUPS, CHUNK), f32),
        "w_branch_a": nrm(ks[11], (DEPTH, POOL_WIDTH, D_MODEL), POOL_WIDTH ** -0.5),
        "w_branch_b": nrm(ks[12], (DEPTH, SGU_WIDTH, D_MODEL), SGU_WIDTH ** -0.5),
        "w_out": nrm(ks[13], (DEPTH, D_MODEL, D_MODEL), D_MODEL ** -0.5),
        "norm_xattn": gain(ks[14], (DEPTH, D_MODEL)),
        "w_xq": nrm(ks[15], (DEPTH, D_MODEL, D_MODEL), D_MODEL ** -0.5),
        "w_xkv": nrm(ks[16], (DEPTH, D_MODEL, 2 * D_MODEL), D_MODEL ** -0.5),
        "w_xo": nrm(ks[17], (DEPTH, D_MODEL, D_MODEL), D_MODEL ** -0.5),
        "norm_ffn": gain(ks[18], (DEPTH, D_MODEL)),
        "w_ff_gu": nrm(ks[19], (N_DENSE, D_MODEL, 2 * D_FF), D_MODEL ** -0.5),
        "w_ff_down": nrm(ks[20], (N_DENSE, D_FF, D_MODEL), D_FF ** -0.5),
        "w_router": nrm(ks[21], (N_MOE, D_MODEL, N_EXPERTS), D_MODEL ** -0.5),
        "w_moe_gu": nrm(ks[22], (N_MOE, N_EXPERTS, D_MODEL, 2 * D_FF_EXPERT), D_MODEL ** -0.5),
        "w_moe_down": nrm(ks[23], (N_MOE, N_EXPERTS, D_FF_EXPERT, D_MODEL), D_FF_EXPERT ** -0.5),
        "norm_final": gain(ks[24], (D_MODEL,)),
    }


def reference(x, mem, norm_mem, norm_mix, w_in, pool_mix, pool_scale, sgu_ln_g, sgu_ln_b,
              w_spatial, b_spatial, w_branch_a, w_branch_b, w_out, norm_xattn, w_xq, w_xkv,
              w_xo, norm_ffn, w_ff_gu, w_ff_down, w_router, w_moe_gu, w_moe_down, norm_final):
    splits = [POOL_WIDTH, POOL_WIDTH + SGU_WIDTH, POOL_WIDTH + 2 * SGU_WIDTH,
              POOL_WIDTH + 2 * SGU_WIDTH + D_MODEL]
    mem_n = rms_norm(mem, norm_mem)
    for l in range(DEPTH):
        h = rms_norm(x, norm_mix[l])
        p, u, v, ga, gb = jnp.split(h @ w_in[l], splits, axis=-1)
        br_a = pool_mixer(p, pool_mix[l], pool_scale[l]) @ w_branch_a[l]
        br_b = spatial_gating(jax.nn.gelu(u), jax.nn.gelu(v), sgu_ln_g[l], sgu_ln_b[l],
                              w_spatial[l], b_spatial[l]) @ w_branch_b[l]
        merged = jax.nn.sigmoid(ga) * br_a + jax.nn.sigmoid(gb) * br_b
        x = x + merged @ w_out[l]
        h = rms_norm(x, norm_xattn[l])
        x = x + cross_attention(h, mem_n, w_xq[l], w_xkv[l], w_xo[l])
        h = rms_norm(x, norm_ffn[l])
        if l % 2 == 0:
            x = x + swiglu(h, w_ff_gu[l // 2], w_ff_down[l // 2])
        else:
            x = x + moe_swiglu(h, w_router[l // 2], w_moe_gu[l // 2], w_moe_down[l // 2])
    return rms_norm(x, norm_final)
```

```python
import functools

import jax
import jax.numpy as jnp
from jax import lax
from jax.experimental import pallas as pl
from jax.experimental.pallas import tpu as pltpu

D_MODEL = 1024
BATCH = 8
SEQ = 4096
DEPTH = 2
MEM_LEN = 256
POOL_WIDTH = 512
POOL_GROUPS = 4
POOL_WINDOWS = (2, 4, 8, 16)
POOL_GROUP_DIM = POOL_WIDTH // POOL_GROUPS
SGU_WIDTH = 512
SGU_GROUPS = 4
SGU_GROUP_DIM = SGU_WIDTH // SGU_GROUPS
CHUNK = 128
IN_COLS = POOL_WIDTH + 2 * SGU_WIDTH + 2 * D_MODEL
XATTN_HEADS = 4
XATTN_HEAD_DIM = D_MODEL // XATTN_HEADS
D_FF = 2816
N_EXPERTS = 8
TOP_K = 2
D_FF_EXPERT = 3584
EPS = 1e-6

N_TOKENS = BATCH * SEQ
LANES = 128
POOL_HALO = 16
TOK_TILE = 512
ROW_TILE = 512
DMA_TILE = 256
N_ROW_TILES = (N_TOKENS * TOP_K) // ROW_TILE + N_EXPERTS
N_SORTED_ROWS = N_ROW_TILES * ROW_TILE
FF_CHUNKS = 2
EXPERT_FF_CHUNKS = 2
VMEM_LIMIT = 52 * 1024 * 1024

BF16 = jnp.bfloat16
F32 = jnp.float32


def _dot(a, b):
    return jnp.dot(a, b, preferred_element_type=F32)


def _rms(x, g):
    return x * lax.rsqrt(jnp.mean(x * x, axis=-1, keepdims=True) + EPS) * g


def _const_spec(shape):
    return pl.BlockSpec(shape, lambda *_: (0,) * len(shape), pipeline_mode=pl.Buffered(1))


def _params(n_axes=1):
    return pltpu.CompilerParams(dimension_semantics=("arbitrary",) * n_axes,
                                vmem_limit_bytes=VMEM_LIMIT)


def _kv_kernel(mem_ref, g_ref, w_ref, o_ref):
    mem_n = _rms(mem_ref[...], g_ref[...]).astype(BF16)
    o_ref[...] = _dot(mem_n, w_ref[...]).astype(BF16)


def _kv(mem2d, norm_mem, w_xkv):
    return pl.pallas_call(
        _kv_kernel,
        grid=(DEPTH, BATCH),
        in_specs=[
            pl.BlockSpec((MEM_LEN, D_MODEL), lambda l, b: (b, 0)),
            pl.BlockSpec((1, D_MODEL), lambda l, b: (0, 0)),
            pl.BlockSpec((None, D_MODEL, 2 * D_MODEL), lambda l, b: (l, 0, 0)),
        ],
        out_specs=pl.BlockSpec((None, MEM_LEN, 2 * D_MODEL), lambda l, b: (l, b, 0)),
        out_shape=jax.ShapeDtypeStruct((DEPTH, BATCH * MEM_LEN, 2 * D_MODEL), BF16),
        compiler_params=_params(2),
        name="kv",
    )(mem2d, norm_mem, w_xkv)


def _mixer_kernel(x_ref, nm_ref, win_ref, pmix_ref, pscale_ref, lng_ref, lnb_ref, wsp_ref,
                  bsp_ref, wa_ref, wb_ref, wout_ref, o_ref, pext_ref, ain_ref, vn_ref, sg_ref):
    T = TOK_TILE
    G = POOL_GROUP_DIM
    tile_in_seq = pl.program_id(0) % (SEQ // T)

    @pl.when(tile_in_seq == 0)
    def _():
        pext_ref[0:POOL_HALO, :] = jnp.zeros((POOL_HALO, POOL_WIDTH), F32)

    x = x_ref[...]
    h = _rms(x, nm_ref[...]).astype(BF16)

    p = _dot(h, win_ref[:, 0:POOL_WIDTH])
    pext_ref[POOL_HALO:POOL_HALO + T, :] = p
    t_pos = tile_in_seq * T + lax.broadcasted_iota(jnp.int32, (T, G), 0)
    for gi, w in enumerate(POOL_WINDOWS):
        cols = slice(gi * G, (gi + 1) * G)
        pg = p[:, cols]
        acc = pg
        for j in range(1, w):
            acc = acc + pext_ref[POOL_HALO - j:POOL_HALO - j + T, cols]
        cnt = jnp.minimum(t_pos + 1, w).astype(F32)
        pooled = acc / cnt - pg
        mixed = _dot(pooled.astype(BF16), pmix_ref[gi])
        ain_ref[:, cols] = (mixed * pscale_ref[:, cols]).astype(BF16)
    pext_ref[0:POOL_HALO, :] = pext_ref[T:T + POOL_HALO, :]
    br_a = _dot(ain_ref[...], wa_ref[...])

    u = jax.nn.gelu(_dot(h, win_ref[:, POOL_WIDTH:POOL_WIDTH + SGU_WIDTH]))
    v = jax.nn.gelu(_dot(h, win_ref[:, POOL_WIDTH + SGU_WIDTH:POOL_WIDTH + 2 * SGU_WIDTH]))
    mu = jnp.mean(v, axis=-1, keepdims=True)
    vc = v - mu
    vn = vc * lax.rsqrt(jnp.mean(vc * vc, axis=-1, keepdims=True) + EPS) * lng_ref[...] + lnb_ref[...]
    vn_ref[...] = vn.astype(BF16)
    causal = (lax.broadcasted_iota(jnp.int32, (CHUNK, CHUNK), 0)
              >= lax.broadcasted_iota(jnp.int32, (CHUNK, CHUNK), 1))
    n_chunks = T // CHUNK
    for gi in range(SGU_GROUPS):
        cols = slice(gi * SGU_GROUP_DIM, (gi + 1) * SGU_GROUP_DIM)
        w_sp = jnp.where(causal, wsp_ref[gi], 0.0).astype(BF16)
        rhs = jnp.concatenate([vn_ref[c * CHUNK:(c + 1) * CHUNK, cols] for c in range(n_chunks)], axis=1)
        mixed = _dot(w_sp, rhs)
        for c in range(n_chunks):
            rows = slice(c * CHUNK, (c + 1) * CHUNK)
            blk = mixed[:, c * SGU_GROUP_DIM:(c + 1) * SGU_GROUP_DIM] + bsp_ref[:, cols]
            sg_ref[rows, cols] = (u[rows, cols] * blk).astype(BF16)
    br_b = _dot(sg_ref[...], wb_ref[...])

    c0 = POOL_WIDTH + 2 * SGU_WIDTH
    ga = _dot(h, win_ref[:, c0:c0 + D_MODEL])
    gb = _dot(h, win_ref[:, c0 + D_MODEL:c0 + 2 * D_MODEL])
    merged = jax.nn.sigmoid(ga) * br_a + jax.nn.sigmoid(gb) * br_b
    o_ref[...] = x + _dot(merged.astype(BF16), wout_ref[...])


def _mixer(x, nm, win, pmix, pscale, lng, lnb, wsp, bsp_tile, wa, wb, wout):
    T = TOK_TILE
    tok = pl.BlockSpec((T, D_MODEL), lambda i: (i, 0))
    return pl.pallas_call(
        _mixer_kernel,
        grid=(N_TOKENS // T,),
        in_specs=[
            tok,
            _const_spec((1, D_MODEL)),
            _const_spec((D_MODEL, IN_COLS)),
            _const_spec((POOL_GROUPS, POOL_GROUP_DIM, POOL_GROUP_DIM)),
            _const_spec((1, POOL_WIDTH)),
            _const_spec((1, SGU_WIDTH)),
            _const_spec((1, SGU_WIDTH)),
            _const_spec((SGU_GROUPS, CHUNK, CHUNK)),
            _const_spec((CHUNK, SGU_WIDTH)),
            _const_spec((POOL_WIDTH, D_MODEL)),
            _const_spec((SGU_WIDTH, D_MODEL)),
            _const_spec((D_MODEL, D_MODEL)),
        ],
        out_specs=tok,
        out_shape=jax.ShapeDtypeStruct((N_TOKENS, D_MODEL), F32),
        scratch_shapes=[
            pltpu.VMEM((POOL_HALO + T, POOL_WIDTH), F32),
            pltpu.VMEM((T, POOL_WIDTH), BF16),
            pltpu.VMEM((T, SGU_WIDTH), BF16),
            pltpu.VMEM((T, SGU_WIDTH), BF16),
        ],
        compiler_params=_params(),
        name="mixer",
    )(x, nm, win, pmix, pscale, lng, lnb, wsp, bsp_tile, wa, wb, wout)


def _xattn_kernel(x_ref, g_ref, wq_ref, kv_ref, wo_ref, o_ref, ocat_ref):
    x = x_ref[...]
    h = _rms(x, g_ref[...]).astype(BF16)
    q = (_dot(h, wq_ref[...]) * (XATTN_HEAD_DIM ** -0.5)).astype(BF16)
    for hd in range(XATTN_HEADS):
        cols = slice(hd * XATTN_HEAD_DIM, (hd + 1) * XATTN_HEAD_DIM)
        k = kv_ref[:, cols]
        v = kv_ref[:, D_MODEL + hd * XATTN_HEAD_DIM:D_MODEL + (hd + 1) * XATTN_HEAD_DIM]
        s = lax.dot_general(q[:, cols], k, (((1,), (1,)), ((), ())), preferred_element_type=F32)
        e = jnp.exp(s - jnp.max(s, axis=-1, keepdims=True))
        pr = e / jnp.sum(e, axis=-1, keepdims=True)
        ocat_ref[:, cols] = _dot(pr.astype(BF16), v).astype(BF16)
    o_ref[...] = x + _dot(ocat_ref[...], wo_ref[...])


def _xattn(x, g, wq, kv_l, wo):
    T = TOK_TILE
    tok = pl.BlockSpec((T, D_MODEL), lambda i: (i, 0))
    return pl.pallas_call(
        _xattn_kernel,
        grid=(N_TOKENS // T,),
        in_specs=[
            tok,
            _const_spec((1, D_MODEL)),
            _const_spec((D_MODEL, D_MODEL)),
            pl.BlockSpec((MEM_LEN, 2 * D_MODEL), lambda i: (i // (SEQ // T), 0)),
            _const_spec((D_MODEL, D_MODEL)),
        ],
        out_specs=tok,
        out_shape=jax.ShapeDtypeStruct((N_TOKENS, D_MODEL), F32),
        scratch_shapes=[pltpu.VMEM((T, D_MODEL), BF16)],
        compiler_params=_params(),
        name="xattn",
    )(x, g, wq, kv_l, wo)


def _swiglu_rows(h, wgu_ref, wd_ref, d_ff, n_chunks):
    fc = d_ff // n_chunks
    acc = None
    for c in range(n_chunks):
        g = _dot(h, wgu_ref[:, c * fc:(c + 1) * fc])
        up = _dot(h, wgu_ref[:, d_ff + c * fc:d_ff + (c + 1) * fc])
        a = (g * jax.nn.sigmoid(g) * up).astype(BF16)
        part = _dot(a, wd_ref[c * fc:(c + 1) * fc, :])
        acc = part if acc is None else acc + part
    return acc


def _ffn_kernel(x_ref, g_ref, wgu_ref, wd_ref, o_ref):
    x = x_ref[...]
    h = _rms(x, g_ref[...]).astype(BF16)
    o_ref[...] = x + _swiglu_rows(h, wgu_ref, wd_ref, D_FF, FF_CHUNKS)


def _ffn(x, g, wgu, wd):
    T = TOK_TILE
    tok = pl.BlockSpec((T, D_MODEL), lambda i: (i, 0))
    return pl.pallas_call(
        _ffn_kernel,
        grid=(N_TOKENS // T,),
        in_specs=[tok, _const_spec((1, D_MODEL)), _const_spec((D_MODEL, 2 * D_FF)),
                  _const_spec((D_FF, D_MODEL))],
        out_specs=tok,
        out_shape=jax.ShapeDtypeStruct((N_TOKENS, D_MODEL), F32),
        compiler_params=_params(),
        name="ffn",
    )(x, g, wgu, wd)


def _router_kernel(x_ref, g_ref, wr_ref, h_ref, meta_ref, gate_ref, cnt_ref, carry_ref):
    T = TOK_TILE

    @pl.when(pl.program_id(0) == 0)
    def _():
        carry_ref[...] = jnp.zeros((1, LANES), F32)

    h = _rms(x_ref[...], g_ref[...])
    h_ref[...] = h
    lane = lax.broadcasted_iota(jnp.int32, (T, LANES), 1)
    logits = jnp.where(lane < N_EXPERTS, _dot(h.astype(BF16), wr_ref[...]), -jnp.inf)
    m1 = jnp.max(logits, axis=-1, keepdims=True)
    i1 = jnp.min(jnp.where(logits == m1, lane, LANES), axis=-1, keepdims=True)
    rest = jnp.where(lane == i1, -jnp.inf, logits)
    m2 = jnp.max(rest, axis=-1, keepdims=True)
    i2 = jnp.min(jnp.where(rest == m2, lane, LANES), axis=-1, keepdims=True)
    e2 = jnp.exp(m2 - m1)
    g1 = 1.0 / (1.0 + e2)
    g2 = e2 / (1.0 + e2)

    sel1 = lane == i1
    sel2 = lane == i2
    onehot = jnp.where(sel1, 1.0, jnp.where(sel2, 1.0, 0.0))
    earlier = (lax.broadcasted_iota(jnp.int32, (T, T), 1)
               < lax.broadcasted_iota(jnp.int32, (T, T), 0))
    before = carry_ref[...] + _dot(jnp.where(earlier, 1.0, 0.0).astype(BF16), onehot.astype(BF16))
    r1 = jnp.sum(jnp.where(sel1, before, 0.0), axis=-1, keepdims=True).astype(jnp.int32)
    r2 = jnp.sum(jnp.where(sel2, before, 0.0), axis=-1, keepdims=True).astype(jnp.int32)
    carry_ref[...] = carry_ref[...] + jnp.sum(onehot, axis=0, keepdims=True)

    meta_ref[...] = jnp.where(lane == 0, i1, jnp.where(lane == 1, i2, jnp.where(lane == 2, r1, r2)))
    gate_ref[...] = jnp.where(lane == 0, g1, g2)
    cnt_ref[...] = carry_ref[...].astype(jnp.int32)


def _router(x, g, wr_pad):
    T = TOK_TILE
    tok = pl.BlockSpec((T, D_MODEL), lambda i: (i, 0))
    slab = pl.BlockSpec((T, LANES), lambda i: (i, 0))
    return pl.pallas_call(
        _router_kernel,
        grid=(N_TOKENS // T,),
        in_specs=[tok, _const_spec((1, D_MODEL)), _const_spec((D_MODEL, LANES))],
        out_specs=[tok, slab, slab, pl.BlockSpec((1, LANES), lambda i: (0, 0))],
        out_shape=[
            jax.ShapeDtypeStruct((N_TOKENS, D_MODEL), F32),
            jax.ShapeDtypeStruct((N_TOKENS, LANES), jnp.int32),
            jax.ShapeDtypeStruct((N_TOKENS, LANES), F32),
            jax.ShapeDtypeStruct((1, LANES), jnp.int32),
        ],
        scratch_shapes=[pltpu.VMEM((1, LANES), F32)],
        compiler_params=_params(),
        name="router",
    )(x, g, wr_pad)


def _row_copy(src, src_row, dst, dst_row, sem):
    return pltpu.make_async_copy(src.at[pl.ds(src_row, 1)], dst.at[pl.ds(dst_row, 1)], sem)


def _dispatch_kernel(offs_ref, meta_ref, h_ref, xs_in_ref, xs_ref, sem):
    del xs_in_ref

    def issue(t, carry):
        for k in range(TOP_K):
            pos = offs_ref[meta_ref[0, 0, 4 * t + k]] + meta_ref[0, 0, 4 * t + 2 + k]
            _row_copy(h_ref, t, xs_ref, pos, sem).start()
        return carry

    lax.fori_loop(0, DMA_TILE, issue, 0)

    def drain(t, carry):
        for k in range(TOP_K):
            _row_copy(h_ref, 0, xs_ref, 0, sem).wait()
        return carry

    lax.fori_loop(0, DMA_TILE, drain, 0)


def _dispatch(offs, meta_smem, h, xs_zero):
    T = DMA_TILE
    grid_spec = pltpu.PrefetchScalarGridSpec(
        num_scalar_prefetch=1,
        grid=(N_TOKENS // T,),
        in_specs=[
            pl.BlockSpec((1, 1, 4 * T), lambda i, offs: (i, 0, 0), memory_space=pltpu.SMEM),
            pl.BlockSpec((T, D_MODEL), lambda i, offs: (i, 0)),
            pl.BlockSpec(memory_space=pl.ANY),
        ],
        out_specs=pl.BlockSpec(memory_space=pl.ANY),
        scratch_shapes=[pltpu.SemaphoreType.DMA(())],
    )
    return pl.pallas_call(
        _dispatch_kernel,
        grid_spec=grid_spec,
        out_shape=jax.ShapeDtypeStruct((N_SORTED_ROWS, D_MODEL), F32),
        input_output_aliases={3: 0},
        compiler_params=_params(),
        name="dispatch",
    )(offs, meta_smem, h, xs_zero)


def _experts_kernel(tile_e_ref, tile_on_ref, xs_ref, wgu_ref, wd_ref, y_ref):
    del tile_e_ref
    on = tile_on_ref[pl.program_id(0)] == 1

    @pl.when(on)
    def _():
        y_ref[...] = _swiglu_rows(xs_ref[...].astype(BF16), wgu_ref, wd_ref, D_FF_EXPERT,
                                  EXPERT_FF_CHUNKS)

    @pl.when(jnp.logical_not(on))
    def _():
        y_ref[...] = jnp.zeros((ROW_TILE, D_MODEL), F32)


def _experts(tile_e, tile_on, xs, wgu, wd):
    rows = pl.BlockSpec((ROW_TILE, D_MODEL), lambda i, te, on: (i, 0))
    grid_spec = pltpu.PrefetchScalarGridSpec(
        num_scalar_prefetch=2,
        grid=(N_ROW_TILES,),
        in_specs=[
            rows,
            pl.BlockSpec((None, D_MODEL, 2 * D_FF_EXPERT), lambda i, te, on: (te[i], 0, 0),
                         pipeline_mode=pl.Buffered(1)),
            pl.BlockSpec((None, D_FF_EXPERT, D_MODEL), lambda i, te, on: (te[i], 0, 0),
                         pipeline_mode=pl.Buffered(1)),
        ],
        out_specs=rows,
    )
    return pl.pallas_call(
        _experts_kernel,
        grid_spec=grid_spec,
        out_shape=jax.ShapeDtypeStruct((N_SORTED_ROWS, D_MODEL), F32),
        compiler_params=_params(),
        name="experts",
    )(tile_e, tile_on, xs, wgu, wd)


def _combine_kernel(offs_ref, meta_ref, x_ref, gate_ref, nf_ref, y_ref, o_ref, buf_ref, sem):
    def issue(t, carry):
        for k in range(TOP_K):
            pos = offs_ref[meta_ref[0, 0, 4 * t + k]] + meta_ref[0, 0, 4 * t + 2 + k]
            _row_copy(y_ref, pos, buf_ref.at[k], t, sem).start()
        return carry

    lax.fori_loop(0, DMA_TILE, issue, 0)

    def drain(t, carry):
        for k in range(TOP_K):
            _row_copy(y_ref, 0, buf_ref.at[k], 0, sem).wait()
        return carry

    lax.fori_loop(0, DMA_TILE, drain, 0)

    gates = gate_ref[...]
    out = x_ref[...] + gates[:, 0:1] * buf_ref[0] + gates[:, 1:2] * buf_ref[1]
    o_ref[...] = _rms(out, nf_ref[...])


def _combine(offs, meta_smem, x, gates, norm_final, y):
    T = DMA_TILE
    grid_spec = pltpu.PrefetchScalarGridSpec(
        num_scalar_prefetch=1,
        grid=(N_TOKENS // T,),
        in_specs=[
            pl.BlockSpec((1, 1, 4 * T), lambda i, offs: (i, 0, 0), memory_space=pltpu.SMEM),
            pl.BlockSpec((T, D_MODEL), lambda i, offs: (i, 0)),
            pl.BlockSpec((T, LANES), lambda i, offs: (i, 0)),
            pl.BlockSpec((1, D_MODEL), lambda i, offs: (0, 0)),
            pl.BlockSpec(memory_space=pl.ANY),
        ],
        out_specs=pl.BlockSpec((T, D_MODEL), lambda i, offs: (i, 0)),
        scratch_shapes=[pltpu.VMEM((TOP_K, T, D_MODEL), F32), pltpu.SemaphoreType.DMA(())],
    )
    return pl.pallas_call(
        _combine_kernel,
        grid_spec=grid_spec,
        out_shape=jax.ShapeDtypeStruct((N_TOKENS, D_MODEL), F32),
        compiler_params=_params(),
        name="combine",
    )(offs, meta_smem, x, gates, norm_final, y)


def _moe_layer(x, norm_ffn, w_router, w_gu, w_down, norm_final):
    wr_pad = jnp.pad(w_router, ((0, 0), (0, LANES - N_EXPERTS))).astype(BF16)
    h, meta, gates, counts = _router(x, norm_ffn, wr_pad)

    counts = counts[0, :N_EXPERTS]
    tiles_per_expert = (counts + ROW_TILE - 1) // ROW_TILE
    tile_end = jnp.cumsum(tiles_per_expert)
    offs = ((tile_end - tiles_per_expert) * ROW_TILE).astype(jnp.int32)
    tile_ids = jnp.arange(N_ROW_TILES, dtype=jnp.int32)
    tile_on = (tile_ids < tile_end[-1]).astype(jnp.int32)
    tile_e = jnp.minimum(jnp.sum(tile_ids[:, None] >= tile_end[None, :], axis=1),
                         N_EXPERTS - 1).astype(jnp.int32)
    tile_e = jnp.where(tile_on == 1, tile_e, tile_e[jnp.maximum(tile_end[-1] - 1, 0)])

    meta_smem = meta[:, :4].reshape(N_TOKENS // DMA_TILE, 1, 4 * DMA_TILE)
    xs = _dispatch(offs, meta_smem, h, jnp.zeros((N_SORTED_ROWS, D_MODEL), F32))
    y = _experts(tile_e, tile_on, xs, w_gu, w_down)
    return _combine(offs, meta_smem, x, gates, norm_final, y)


def kernel(x, mem, norm_mem, norm_mix, w_in, pool_mix, pool_scale, sgu_ln_g, sgu_ln_b, w_spatial, b_spatial, w_branch_a, w_branch_b, w_out, norm_xattn, w_xq, w_xkv, w_xo, norm_ffn, w_ff_gu, w_ff_down, w_router, w_moe_gu, w_moe_down, norm_final):
    assert DEPTH == 2 and x.shape == (BATCH, SEQ, D_MODEL)
    row = lambda a: a.reshape(1, -1)
    bf = lambda a: a.astype(BF16)

    xt = x.reshape(N_TOKENS, D_MODEL)
    kv = _kv(mem.reshape(BATCH * MEM_LEN, D_MODEL), row(norm_mem), bf(w_xkv))
    for l in range(DEPTH):
        bsp_tile = jnp.repeat(b_spatial[l].T, SGU_GROUP_DIM, axis=1)
        xt = _mixer(xt, row(norm_mix[l]), bf(w_in[l]), bf(pool_mix[l]), row(pool_scale[l]),
                    row(sgu_ln_g[l]), row(sgu_ln_b[l]), w_spatial[l], bsp_tile,
                    bf(w_branch_a[l]), bf(w_branch_b[l]), bf(w_out[l]))
        xt = _xattn(xt, row(norm_xattn[l]), bf(w_xq[l]), kv[l], bf(w_xo[l]))
        if l % 2 == 0:
            xt = _ffn(xt, row(norm_ffn[l]), bf(w_ff_gu[l // 2]), bf(w_ff_down[l // 2]))
        else:
            xt = _moe_layer(xt, row(norm_ffn[l]), w_router[l // 2], bf(w_moe_gu[l // 2]),
                            bf(w_moe_down[l // 2]), row(norm_final))
    return xt.reshape(BATCH, SEQ, D_MODEL)
```

```python
import functools

import jax
import jax.numpy as jnp
from jax import lax
from jax.experimental import pallas as pl
from jax.experimental.pallas import tpu as pltpu

D_MODEL = 1024
BATCH = 8
SEQ = 4096
DEPTH = 2
MEM_LEN = 256
POOL_WIDTH = 512
POOL_GROUPS = 4
POOL_WINDOWS = (2, 4, 8, 16)
POOL_GROUP_DIM = POOL_WIDTH // POOL_GROUPS
SGU_WIDTH = 512
SGU_GROUPS = 4
SGU_GROUP_DIM = SGU_WIDTH // SGU_GROUPS
CHUNK = 128
IN_COLS = POOL_WIDTH + 2 * SGU_WIDTH + 2 * D_MODEL
XATTN_HEADS = 4
XATTN_HEAD_DIM = D_MODEL // XATTN_HEADS
D_FF = 2816
N_EXPERTS = 8
TOP_K = 2
D_FF_EXPERT = 3584
EPS = 1e-6

N_TOKENS = BATCH * SEQ
LANES = 128
POOL_HALO = 16
TOK_TILE = 512
ROW_TILE = 512
SUBLANES = 8
N_BLOCKS = N_TOKENS // TOK_TILE
BLOCK_ROWS = -(-(TOK_TILE * TOP_K + N_EXPERTS * (SUBLANES - 1)) // LANES) * LANES
MAX_SORTED = N_TOKENS * TOP_K + N_BLOCKS * N_EXPERTS * (SUBLANES - 1)
N_ROW_TILES = -(-MAX_SORTED // ROW_TILE) + N_EXPERTS
N_SORTED_ROWS = N_ROW_TILES * ROW_TILE
FF_CHUNKS = 2
EXPERT_FF_CHUNKS = 2
VMEM_LIMIT = 52 * 1024 * 1024

BF16 = jnp.bfloat16
F32 = jnp.float32


def _dot(a, b):
    return jnp.dot(a, b, preferred_element_type=F32)


def _rms(x, g):
    return x * lax.rsqrt(jnp.mean(x * x, axis=-1, keepdims=True) + EPS) * g


def _const_spec(shape):
    return pl.BlockSpec(shape, lambda *_: (0,) * len(shape), pipeline_mode=pl.Buffered(1))


def _params(n_axes=1):
    return pltpu.CompilerParams(dimension_semantics=("arbitrary",) * n_axes,
                                vmem_limit_bytes=VMEM_LIMIT)


def _kv_kernel(mem_ref, g_ref, w_ref, o_ref):
    mem_n = _rms(mem_ref[...], g_ref[...]).astype(BF16)
    o_ref[...] = _dot(mem_n, w_ref[...]).astype(BF16)


def _kv(mem2d, norm_mem, w_xkv):
    return pl.pallas_call(
        _kv_kernel,
        grid=(DEPTH, BATCH),
        in_specs=[
            pl.BlockSpec((MEM_LEN, D_MODEL), lambda l, b: (b, 0)),
            pl.BlockSpec((1, D_MODEL), lambda l, b: (0, 0)),
            pl.BlockSpec((None, D_MODEL, 2 * D_MODEL), lambda l, b: (l, 0, 0)),
        ],
        out_specs=pl.BlockSpec((None, MEM_LEN, 2 * D_MODEL), lambda l, b: (l, b, 0)),
        out_shape=jax.ShapeDtypeStruct((DEPTH, BATCH * MEM_LEN, 2 * D_MODEL), BF16),
        compiler_params=_params(2),
        name="kv",
    )(mem2d, norm_mem, w_xkv)


def _mixer_kernel(x_ref, nm_ref, win_ref, pmix_ref, pscale_ref, lng_ref, lnb_ref, wsp_ref,
                  bsp_ref, wa_ref, wb_ref, wout_ref, o_ref, pext_ref, ain_ref, vn_ref, sg_ref):
    T = TOK_TILE
    G = POOL_GROUP_DIM
    tile_in_seq = pl.program_id(0) % (SEQ // T)

    @pl.when(tile_in_seq == 0)
    def _():
        pext_ref[0:POOL_HALO, :] = jnp.zeros((POOL_HALO, POOL_WIDTH), F32)

    x = x_ref[...]
    h = _rms(x, nm_ref[...]).astype(BF16)

    p = _dot(h, win_ref[:, 0:POOL_WIDTH])
    pext_ref[POOL_HALO:POOL_HALO + T, :] = p
    t_pos = tile_in_seq * T + lax.broadcasted_iota(jnp.int32, (T, G), 0)
    for gi, w in enumerate(POOL_WINDOWS):
        cols = slice(gi * G, (gi + 1) * G)
        pg = p[:, cols]
        acc = pg
        for j in range(1, w):
            acc = acc + pext_ref[POOL_HALO - j:POOL_HALO - j + T, cols]
        cnt = jnp.minimum(t_pos + 1, w).astype(F32)
        pooled = acc / cnt - pg
        mixed = _dot(pooled.astype(BF16), pmix_ref[gi])
        ain_ref[:, cols] = (mixed * pscale_ref[:, cols]).astype(BF16)
    pext_ref[0:POOL_HALO, :] = pext_ref[T:T + POOL_HALO, :]
    br_a = _dot(ain_ref[...], wa_ref[...])

    u = jax.nn.gelu(_dot(h, win_ref[:, POOL_WIDTH:POOL_WIDTH + SGU_WIDTH]))
    v = jax.nn.gelu(_dot(h, win_ref[:, POOL_WIDTH + SGU_WIDTH:POOL_WIDTH + 2 * SGU_WIDTH]))
    mu = jnp.mean(v, axis=-1, keepdims=True)
    vc = v - mu
    vn = vc * lax.rsqrt(jnp.mean(vc * vc, axis=-1, keepdims=True) + EPS) * lng_ref[...] + lnb_ref[...]
    vn_ref[...] = vn.astype(BF16)
    causal = (lax.broadcasted_iota(jnp.int32, (CHUNK, CHUNK), 0)
              >= lax.broadcasted_iota(jnp.int32, (CHUNK, CHUNK), 1))
    n_chunks = T // CHUNK
    for gi in range(SGU_GROUPS):
        cols = slice(gi * SGU_GROUP_DIM, (gi + 1) * SGU_GROUP_DIM)
        w_sp = jnp.where(causal, wsp_ref[gi], 0.0).astype(BF16)
        rhs = jnp.concatenate([vn_ref[c * CHUNK:(c + 1) * CHUNK, cols] for c in range(n_chunks)], axis=1)
        mixed = _dot(w_sp, rhs)
        for c in range(n_chunks):
            rows = slice(c * CHUNK, (c + 1) * CHUNK)
            blk = mixed[:, c * SGU_GROUP_DIM:(c + 1) * SGU_GROUP_DIM] + bsp_ref[:, cols]
            sg_ref[rows, cols] = (u[rows, cols] * blk).astype(BF16)
    br_b = _dot(sg_ref[...], wb_ref[...])

    c0 = POOL_WIDTH + 2 * SGU_WIDTH
    ga = _dot(h, win_ref[:, c0:c0 + D_MODEL])
    gb = _dot(h, win_ref[:, c0 + D_MODEL:c0 + 2 * D_MODEL])
    merged = jax.nn.sigmoid(ga) * br_a + jax.nn.sigmoid(gb) * br_b
    o_ref[...] = x + _dot(merged.astype(BF16), wout_ref[...])


def _mixer(x, nm, win, pmix, pscale, lng, lnb, wsp, bsp_tile, wa, wb, wout):
    T = TOK_TILE
    tok = pl.BlockSpec((T, D_MODEL), lambda i: (i, 0))
    return pl.pallas_call(
        _mixer_kernel,
        grid=(N_TOKENS // T,),
        in_specs=[
            tok,
            _const_spec((1, D_MODEL)),
            _const_spec((D_MODEL, IN_COLS)),
            _const_spec((POOL_GROUPS, POOL_GROUP_DIM, POOL_GROUP_DIM)),
            _const_spec((1, POOL_WIDTH)),
            _const_spec((1, SGU_WIDTH)),
            _const_spec((1, SGU_WIDTH)),
            _const_spec((SGU_GROUPS, CHUNK, CHUNK)),
            _const_spec((CHUNK, SGU_WIDTH)),
            _const_spec((POOL_WIDTH, D_MODEL)),
            _const_spec((SGU_WIDTH, D_MODEL)),
            _const_spec((D_MODEL, D_MODEL)),
        ],
        out_specs=tok,
        out_shape=jax.ShapeDtypeStruct((N_TOKENS, D_MODEL), F32),
        scratch_shapes=[
            pltpu.VMEM((POOL_HALO + T, POOL_WIDTH), F32),
            pltpu.VMEM((T, POOL_WIDTH), BF16),
            pltpu.VMEM((T, SGU_WIDTH), BF16),
            pltpu.VMEM((T, SGU_WIDTH), BF16),
        ],
        compiler_params=_params(),
        name="mixer",
    )(x, nm, win, pmix, pscale, lng, lnb, wsp, bsp_tile, wa, wb, wout)


def _xattn_kernel(x_ref, g_ref, wq_ref, kv_ref, wo_ref, o_ref, ocat_ref):
    x = x_ref[...]
    h = _rms(x, g_ref[...]).astype(BF16)
    q = (_dot(h, wq_ref[...]) * (XATTN_HEAD_DIM ** -0.5)).astype(BF16)
    for hd in range(XATTN_HEADS):
        cols = slice(hd * XATTN_HEAD_DIM, (hd + 1) * XATTN_HEAD_DIM)
        k = kv_ref[:, cols]
        v = kv_ref[:, D_MODEL + hd * XATTN_HEAD_DIM:D_MODEL + (hd + 1) * XATTN_HEAD_DIM]
        s = lax.dot_general(q[:, cols], k, (((1,), (1,)), ((), ())), preferred_element_type=F32)
        e = jnp.exp(s - jnp.max(s, axis=-1, keepdims=True))
        pr = e / jnp.sum(e, axis=-1, keepdims=True)
        ocat_ref[:, cols] = _dot(pr.astype(BF16), v).astype(BF16)
    o_ref[...] = x + _dot(ocat_ref[...], wo_ref[...])


def _xattn(x, g, wq, kv_l, wo):
    T = TOK_TILE
    tok = pl.BlockSpec((T, D_MODEL), lambda i: (i, 0))
    return pl.pallas_call(
        _xattn_kernel,
        grid=(N_TOKENS // T,),
        in_specs=[
            tok,
            _const_spec((1, D_MODEL)),
            _const_spec((D_MODEL, D_MODEL)),
            pl.BlockSpec((MEM_LEN, 2 * D_MODEL), lambda i: (i // (SEQ // T), 0)),
            _const_spec((D_MODEL, D_MODEL)),
        ],
        out_specs=tok,
        out_shape=jax.ShapeDtypeStruct((N_TOKENS, D_MODEL), F32),
        scratch_shapes=[pltpu.VMEM((T, D_MODEL), BF16)],
        compiler_params=_params(),
        name="xattn",
    )(x, g, wq, kv_l, wo)


def _swiglu_rows(h, wgu_ref, wd_ref, d_ff, n_chunks):
    fc = d_ff // n_chunks
    acc = None
    for c in range(n_chunks):
        g = _dot(h, wgu_ref[:, c * fc:(c + 1) * fc])
        up = _dot(h, wgu_ref[:, d_ff + c * fc:d_ff + (c + 1) * fc])
        a = (g * jax.nn.sigmoid(g) * up).astype(BF16)
        part = _dot(a, wd_ref[c * fc:(c + 1) * fc, :])
        acc = part if acc is None else acc + part
    return acc


def _ffn_kernel(x_ref, g_ref, wgu_ref, wd_ref, o_ref):
    x = x_ref[...]
    h = _rms(x, g_ref[...]).astype(BF16)
    o_ref[...] = x + _swiglu_rows(h, wgu_ref, wd_ref, D_FF, FF_CHUNKS)


def _ffn(x, g, wgu, wd):
    T = TOK_TILE
    tok = pl.BlockSpec((T, D_MODEL), lambda i: (i, 0))
    return pl.pallas_call(
        _ffn_kernel,
        grid=(N_TOKENS // T,),
        in_specs=[tok, _const_spec((1, D_MODEL)), _const_spec((D_MODEL, 2 * D_FF)),
                  _const_spec((D_FF, D_MODEL))],
        out_specs=tok,
        out_shape=jax.ShapeDtypeStruct((N_TOKENS, D_MODEL), F32),
        compiler_params=_params(),
        name="ffn",
    )(x, g, wgu, wd)


def _router_kernel(x_ref, g_ref, wr_ref, h_ref, meta_ref, gate_ref, cnt_ref):
    T = TOK_TILE
    h = _rms(x_ref[...], g_ref[...]).astype(BF16)
    h_ref[...] = h
    lane = lax.broadcasted_iota(jnp.int32, (T, LANES), 1)
    logits = jnp.where(lane < N_EXPERTS, _dot(h, wr_ref[...]), -jnp.inf)
    m1 = jnp.max(logits, axis=-1, keepdims=True)
    i1 = jnp.min(jnp.where(logits == m1, lane, LANES), axis=-1, keepdims=True)
    rest = jnp.where(lane == i1, -jnp.inf, logits)
    m2 = jnp.max(rest, axis=-1, keepdims=True)
    i2 = jnp.min(jnp.where(rest == m2, lane, LANES), axis=-1, keepdims=True)
    e2 = jnp.exp(m2 - m1)
    g1 = 1.0 / (1.0 + e2)
    g2 = e2 / (1.0 + e2)

    sel1 = lane == i1
    sel2 = lane == i2
    onehot = jnp.where(sel1, 1.0, jnp.where(sel2, 1.0, 0.0))
    earlier = (lax.broadcasted_iota(jnp.int32, (T, T), 1)
               < lax.broadcasted_iota(jnp.int32, (T, T), 0))
    before = _dot(jnp.where(earlier, 1.0, 0.0).astype(BF16), onehot.astype(BF16))
    r1 = jnp.sum(jnp.where(sel1, before, 0.0), axis=-1, keepdims=True).astype(jnp.int32)
    r2 = jnp.sum(jnp.where(sel2, before, 0.0), axis=-1, keepdims=True).astype(jnp.int32)

    meta_ref[...] = jnp.where(lane == 0, i1, jnp.where(lane == 1, i2, jnp.where(lane == 2, r1, r2)))
    gate_ref[...] = jnp.where(lane == 0, g1, g2)
    cnt_ref[...] = jnp.sum(onehot, axis=0, keepdims=True).astype(jnp.int32)


def _router(x, g, wr_pad):
    T = TOK_TILE
    tok = pl.BlockSpec((T, D_MODEL), lambda i: (i, 0))
    slab = pl.BlockSpec((T, LANES), lambda i: (i, 0))
    return pl.pallas_call(
        _router_kernel,
        grid=(N_BLOCKS,),
        in_specs=[tok, _const_spec((1, D_MODEL)), _const_spec((D_MODEL, LANES))],
        out_specs=[tok, slab, slab, pl.BlockSpec((None, 1, LANES), lambda i: (i, 0, 0))],
        out_shape=[
            jax.ShapeDtypeStruct((N_TOKENS, D_MODEL), BF16),
            jax.ShapeDtypeStruct((N_TOKENS, LANES), jnp.int32),
            jax.ShapeDtypeStruct((N_TOKENS, LANES), F32),
            jax.ShapeDtypeStruct((N_BLOCKS, 1, LANES), jnp.int32),
        ],
        compiler_params=_params(),
        name="router",
    )(x, g, wr_pad)


def _segment_copies(src, src_row, dst, dst_row, n_rows, max_rows, sem, action):
    size = max_rows
    while size >= SUBLANES:
        done = pl.multiple_of((n_rows // (2 * size)) * (2 * size), SUBLANES)

        @pl.when((n_rows & size) != 0)
        def _(size=size, done=done):
            src_at = 0 if src_row is None else src_row + done
            copy = pltpu.make_async_copy(src.at[pl.ds(src_at, size)],
                                         dst.at[pl.ds(dst_row + done, size)], sem)
            copy.start() if action == "start" else copy.wait()

        size //= 2


def _block_positions(meta_ref, lstart_ref):
    lane = lax.broadcasted_iota(jnp.int32, (TOK_TILE, LANES), 1)
    meta = meta_ref[...]
    lstart = lstart_ref[...]
    pos = []
    for k in range(TOP_K):
        start = jnp.sum(jnp.where(lane == meta[:, k:k + 1], lstart, 0), axis=-1, keepdims=True)
        pos.append(start + meta[:, TOP_K + k:TOP_K + k + 1])
    return pos


def _dispatch_kernel(lstart_s, goff_s, pcnt_s, tail_off_s, tail_len_s, meta_ref, lstart_ref, h_ref,
                     xs_ref, blk_ref, zero_ref, sem):
    b = pl.program_id(0)

    @pl.when(b == 0)
    def _():
        zero_ref[...] = jnp.zeros(zero_ref.shape, F32)
        fill_rows = zero_ref.shape[0]

        def unused_tiles(action):
            def body(i, carry):
                copy = pltpu.make_async_copy(
                    zero_ref, xs_ref.at[pl.ds(pl.multiple_of(i * fill_rows, SUBLANES), fill_rows)], sem)
                copy.start() if action == "start" else copy.wait()
                return carry
            lax.fori_loop(tail_off_s[N_EXPERTS] * (ROW_TILE // fill_rows),
                          N_ROW_TILES * (ROW_TILE // fill_rows), body, 0)

        for action in ("start", "wait"):
            for e in range(N_EXPERTS):
                _segment_copies(zero_ref, None, xs_ref, pl.multiple_of(tail_off_s[e], SUBLANES),
                                tail_len_s[e], fill_rows, sem, action)
            unused_tiles(action)

    p0, p1 = _block_positions(meta_ref, lstart_ref)
    r = lax.broadcasted_iota(jnp.int32, (TOK_TILE, BLOCK_ROWS), 1)
    sel_t = jnp.where(r == p0, 1.0, jnp.where(r == p1, 1.0, 0.0)).astype(BF16)
    blk_ref[...] = lax.dot_general(sel_t, h_ref[...], (((0,), (0,)), ((), ())),
                                   preferred_element_type=F32)

    for action in ("start", "wait"):
        for e in range(N_EXPERTS):
            _segment_copies(blk_ref, pl.multiple_of(lstart_s[b * N_EXPERTS + e], SUBLANES),
                            xs_ref, pl.multiple_of(goff_s[b * N_EXPERTS + e], SUBLANES),
                            pcnt_s[b * N_EXPERTS + e], TOK_TILE, sem, action)


def _dispatch(lstart, goff, pcnt, tail_off, tail_len, meta, lstart_rows, h):
    T = TOK_TILE
    grid_spec = pltpu.PrefetchScalarGridSpec(
        num_scalar_prefetch=5,
        grid=(N_BLOCKS,),
        in_specs=[
            pl.BlockSpec((T, LANES), lambda i, *_: (i, 0)),
            pl.BlockSpec((None, 1, LANES), lambda i, *_: (i, 0, 0)),
            pl.BlockSpec((T, D_MODEL), lambda i, *_: (i, 0)),
        ],
        out_specs=pl.BlockSpec(memory_space=pl.ANY),
        scratch_shapes=[pltpu.VMEM((BLOCK_ROWS, D_MODEL), F32),
                        pltpu.VMEM((ROW_TILE // 2, D_MODEL), F32),
                        pltpu.SemaphoreType.DMA(())],
    )
    return pl.pallas_call(
        _dispatch_kernel,
        grid_spec=grid_spec,
        out_shape=jax.ShapeDtypeStruct((N_SORTED_ROWS, D_MODEL), F32),
        compiler_params=_params(),
        name="dispatch",
    )(lstart, goff, pcnt, tail_off, tail_len, meta, lstart_rows, h)


def _experts_kernel(tile_e_ref, tile_src_ref, xs_ref, wgu_ref, wd_ref, y_ref):
    del tile_e_ref
    on = tile_src_ref[pl.program_id(0)] == pl.program_id(0)

    @pl.when(on)
    def _():
        y_ref[...] = _swiglu_rows(xs_ref[...].astype(BF16), wgu_ref, wd_ref, D_FF_EXPERT,
                                  EXPERT_FF_CHUNKS)

    @pl.when(jnp.logical_not(on))
    def _():
        y_ref[...] = jnp.zeros((ROW_TILE, D_MODEL), F32)


def _experts(tile_e, tile_src, xs, wgu, wd):
    grid_spec = pltpu.PrefetchScalarGridSpec(
        num_scalar_prefetch=2,
        grid=(N_ROW_TILES,),
        in_specs=[
            pl.BlockSpec((ROW_TILE, D_MODEL), lambda i, te, src: (src[i], 0)),
            pl.BlockSpec((None, D_MODEL, 2 * D_FF_EXPERT), lambda i, te, src: (te[i], 0, 0),
                         pipeline_mode=pl.Buffered(1)),
            pl.BlockSpec((None, D_FF_EXPERT, D_MODEL), lambda i, te, src: (te[i], 0, 0),
                         pipeline_mode=pl.Buffered(1)),
        ],
        out_specs=pl.BlockSpec((ROW_TILE, D_MODEL), lambda i, te, src: (i, 0)),
    )
    return pl.pallas_call(
        _experts_kernel,
        grid_spec=grid_spec,
        out_shape=jax.ShapeDtypeStruct((N_SORTED_ROWS, D_MODEL), F32),
        compiler_params=_params(),
        name="experts",
    )(tile_e, tile_src, xs, wgu, wd)


def _combine_kernel(lstart_s, goff_s, pcnt_s, meta_ref, lstart_ref, x_ref, gate_ref, nf_ref, y_ref,
                    o_ref, buf_ref, sem):
    b = pl.program_id(0)
    buf_ref[...] = jnp.zeros(buf_ref.shape, F32)
    for action in ("start", "wait"):
        for e in range(N_EXPERTS):
            _segment_copies(y_ref, pl.multiple_of(goff_s[b * N_EXPERTS + e], SUBLANES),
                            buf_ref, pl.multiple_of(lstart_s[b * N_EXPERTS + e], SUBLANES),
                            pcnt_s[b * N_EXPERTS + e], TOK_TILE, sem, action)

    p0, p1 = _block_positions(meta_ref, lstart_ref)
    r = lax.broadcasted_iota(jnp.int32, (TOK_TILE, BLOCK_ROWS), 1)
    gates = gate_ref[...]
    g0, g1 = gates[:, 0:1], gates[:, 1:2]
    y = buf_ref[...]
    y_hi = y.astype(BF16)
    y_lo = (y - y_hi.astype(F32)).astype(BF16)
    pick0 = jnp.where(r == p0, 1.0, 0.0).astype(BF16)
    pick1 = jnp.where(r == p1, 1.0, 0.0).astype(BF16)
    gated = jnp.where(r == p0, g0, jnp.where(r == p1, g1, 0.0)).astype(BF16)
    moe = g0 * _dot(pick0, y_hi) + g1 * _dot(pick1, y_hi) + _dot(gated, y_lo)
    o_ref[...] = _rms(x_ref[...] + moe, nf_ref[...])


def _combine(lstart, goff, pcnt, meta, lstart_rows, x, gates, norm_final, y):
    T = TOK_TILE
    grid_spec = pltpu.PrefetchScalarGridSpec(
        num_scalar_prefetch=3,
        grid=(N_BLOCKS,),
        in_specs=[
            pl.BlockSpec((T, LANES), lambda i, *_: (i, 0)),
            pl.BlockSpec((None, 1, LANES), lambda i, *_: (i, 0, 0)),
            pl.BlockSpec((T, D_MODEL), lambda i, *_: (i, 0)),
            pl.BlockSpec((T, LANES), lambda i, *_: (i, 0)),
            pl.BlockSpec((1, D_MODEL), lambda i, *_: (0, 0)),
            pl.BlockSpec(memory_space=pl.ANY),
        ],
        out_specs=pl.BlockSpec((T, D_MODEL), lambda i, *_: (i, 0)),
        scratch_shapes=[pltpu.VMEM((BLOCK_ROWS, D_MODEL), F32), pltpu.SemaphoreType.DMA(())],
    )
    return pl.pallas_call(
        _combine_kernel,
        grid_spec=grid_spec,
        out_shape=jax.ShapeDtypeStruct((N_TOKENS, D_MODEL), F32),
        compiler_params=_params(),
        name="combine",
    )(lstart, goff, pcnt, meta, lstart_rows, x, gates, norm_final, y)


def _moe_layer(x, norm_ffn, w_router, w_gu, w_down, norm_final):
    wr_pad = jnp.pad(w_router, ((0, 0), (0, LANES - N_EXPERTS))).astype(BF16)
    h, meta, gates, counts = _router(x, norm_ffn, wr_pad)

    i32 = lambda a: a.astype(jnp.int32)
    pcnt = -(-counts[:, 0, :N_EXPERTS] // SUBLANES) * SUBLANES
    lstart = jnp.cumsum(pcnt, axis=1) - pcnt
    rows_e = jnp.sum(pcnt, axis=0)
    tiles_e = -(-rows_e // ROW_TILE)
    tile_end = jnp.cumsum(tiles_e)
    region = (tile_end - tiles_e) * ROW_TILE
    goff = region[None, :] + jnp.cumsum(pcnt, axis=0) - pcnt
    tail_off = jnp.concatenate([region + rows_e, tile_end[-1:]])
    tail_len = tiles_e * ROW_TILE - rows_e
    tile_ids = jnp.arange(N_ROW_TILES, dtype=jnp.int32)
    tile_src = jnp.minimum(tile_ids, jnp.maximum(tile_end[-1] - 1, 0))
    tile_e = jnp.minimum(jnp.sum(tile_src[:, None] >= tile_end[None, :], axis=1), N_EXPERTS - 1)
    lstart_rows = jnp.pad(lstart, ((0, 0), (0, LANES - N_EXPERTS)))[:, None, :]
    flat = lambda a: i32(a).reshape(-1)

    xs = _dispatch(flat(lstart), flat(goff), flat(pcnt), i32(tail_off), i32(tail_len), meta,
                   i32(lstart_rows), h)
    y = _experts(i32(tile_e), i32(tile_src), xs, w_gu, w_down)
    return _combine(flat(lstart), flat(goff), flat(pcnt), meta, i32(lstart_rows), x, gates,
                    norm_final, y)


def kernel(x, mem, norm_mem, norm_mix, w_in, pool_mix, pool_scale, sgu_ln_g, sgu_ln_b, w_spatial, b_spatial, w_branch_a, w_branch_b, w_out, norm_xattn, w_xq, w_xkv, w_xo, norm_ffn, w_ff_gu, w_ff_down, w_router, w_moe_gu, w_moe_down, norm_final):
    assert DEPTH == 2 and x.shape == (BATCH, SEQ, D_MODEL)
    row = lambda a: a.reshape(1, -1)
    bf = lambda a: a.astype(BF16)

    xt = x.reshape(N_TOKENS, D_MODEL)
    kv = _kv(mem.reshape(BATCH * MEM_LEN, D_MODEL), row(norm_mem), bf(w_xkv))
    for l in range(DEPTH):
        bsp_tile = jnp.repeat(b_spatial[l].T, SGU_GROUP_DIM, axis=1)
        xt = _mixer(xt, row(norm_mix[l]), bf(w_in[l]), bf(pool_mix[l]), row(pool_scale[l]),
                    row(sgu_ln_g[l]), row(sgu_ln_b[l]), w_spatial[l], bsp_tile,
                    bf(w_branch_a[l]), bf(w_branch_b[l]), bf(w_out[l]))
        xt = _xattn(xt, row(norm_xattn[l]), bf(w_xq[l]), kv[l], bf(w_xo[l]))
        if l % 2 == 0:
            xt = _ffn(xt, row(norm_ffn[l]), bf(w_ff_gu[l // 2]), bf(w_ff_down[l // 2]))
        else:
            xt = _moe_layer(xt, row(norm_ffn[l]), w_router[l // 2], bf(w_moe_gu[l // 2]),
                            bf(w_moe_down[l // 2]), row(norm_final))
    return xt.reshape(BATCH, SEQ, D_MODEL)
```

```python
import functools

import jax
import jax.numpy as jnp
from jax import lax
from jax.experimental import pallas as pl
from jax.experimental.pallas import tpu as pltpu

D_MODEL = 1024
BATCH = 8
SEQ = 4096
DEPTH = 2
MEM_LEN = 256
POOL_WIDTH = 512
POOL_GROUPS = 4
POOL_WINDOWS = (2, 4, 8, 16)
POOL_GROUP_DIM = POOL_WIDTH // POOL_GROUPS
SGU_WIDTH = 512
SGU_GROUPS = 4
SGU_GROUP_DIM = SGU_WIDTH // SGU_GROUPS
CHUNK = 128
IN_COLS = POOL_WIDTH + 2 * SGU_WIDTH + 2 * D_MODEL
XATTN_HEADS = 4
XATTN_HEAD_DIM = D_MODEL // XATTN_HEADS
D_FF = 2816
N_EXPERTS = 8
TOP_K = 2
D_FF_EXPERT = 3584
EPS = 1e-6

N_TOKENS = BATCH * SEQ
LANES = 128
POOL_HALO = 16
TOK_TILE = 512
ROW_TILE = 512
SUBLANES = 8
N_BLOCKS = N_TOKENS // TOK_TILE
BLOCK_ROWS = -(-(TOK_TILE * TOP_K + N_EXPERTS * (SUBLANES - 1)) // LANES) * LANES
MAX_SORTED = N_TOKENS * TOP_K + N_BLOCKS * N_EXPERTS * (SUBLANES - 1)
N_ROW_TILES = -(-MAX_SORTED // ROW_TILE) + N_EXPERTS
N_SORTED_ROWS = N_ROW_TILES * ROW_TILE
FF_CHUNKS = 2
EXPERT_FF_CHUNKS = 2
VMEM_LIMIT = 52 * 1024 * 1024

BF16 = jnp.bfloat16
F32 = jnp.float32


def _dot(a, b):
    return jnp.dot(a, b, preferred_element_type=F32)


def _rms(x, g):
    return x * lax.rsqrt(jnp.mean(x * x, axis=-1, keepdims=True) + EPS) * g


def _const_spec(shape):
    return pl.BlockSpec(shape, lambda *_: (0,) * len(shape), pipeline_mode=pl.Buffered(1))


def _params(n_axes=1):
    return pltpu.CompilerParams(dimension_semantics=("arbitrary",) * n_axes,
                                vmem_limit_bytes=VMEM_LIMIT)


def _kv_kernel(mem_ref, g_ref, w_ref, o_ref):
    mem_n = _rms(mem_ref[...], g_ref[...]).astype(BF16)
    o_ref[...] = _dot(mem_n, w_ref[...]).astype(BF16)


def _kv(mem2d, norm_mem, w_xkv):
    return pl.pallas_call(
        _kv_kernel,
        grid=(DEPTH, BATCH),
        in_specs=[
            pl.BlockSpec((MEM_LEN, D_MODEL), lambda l, b: (b, 0)),
            pl.BlockSpec((1, D_MODEL), lambda l, b: (0, 0)),
            pl.BlockSpec((None, D_MODEL, 2 * D_MODEL), lambda l, b: (l, 0, 0)),
        ],
        out_specs=pl.BlockSpec((None, MEM_LEN, 2 * D_MODEL), lambda l, b: (l, b, 0)),
        out_shape=jax.ShapeDtypeStruct((DEPTH, BATCH * MEM_LEN, 2 * D_MODEL), BF16),
        compiler_params=_params(2),
        name="kv",
    )(mem2d, norm_mem, w_xkv)


def _mixer_kernel(x_ref, nm_ref, win_ref, pmix_ref, pscale_ref, lng_ref, lnb_ref, wsp_ref,
                  bsp_ref, wa_ref, wb_ref, wout_ref, o_ref, pext_ref, ain_ref, vn_ref, sg_ref):
    T = TOK_TILE
    G = POOL_GROUP_DIM
    tile_in_seq = pl.program_id(0) % (SEQ // T)

    @pl.when(tile_in_seq == 0)
    def _():
        pext_ref[0:POOL_HALO, :] = jnp.zeros((POOL_HALO, POOL_WIDTH), F32)

    x = x_ref[...]
    h = _rms(x, nm_ref[...]).astype(BF16)

    p = _dot(h, win_ref[:, 0:POOL_WIDTH])
    pext_ref[POOL_HALO:POOL_HALO + T, :] = p
    t_pos = tile_in_seq * T + lax.broadcasted_iota(jnp.int32, (T, G), 0)
    for gi, w in enumerate(POOL_WINDOWS):
        cols = slice(gi * G, (gi + 1) * G)
        pg = p[:, cols]
        acc = pg
        for j in range(1, w):
            acc = acc + pext_ref[POOL_HALO - j:POOL_HALO - j + T, cols]
        cnt = jnp.minimum(t_pos + 1, w).astype(F32)
        pooled = acc / cnt - pg
        mixed = _dot(pooled.astype(BF16), pmix_ref[gi])
        ain_ref[:, cols] = (mixed * pscale_ref[:, cols]).astype(BF16)
    pext_ref[0:POOL_HALO, :] = pext_ref[T:T + POOL_HALO, :]
    br_a = _dot(ain_ref[...], wa_ref[...])

    u = jax.nn.gelu(_dot(h, win_ref[:, POOL_WIDTH:POOL_WIDTH + SGU_WIDTH]))
    v = jax.nn.gelu(_dot(h, win_ref[:, POOL_WIDTH + SGU_WIDTH:POOL_WIDTH + 2 * SGU_WIDTH]))
    mu = jnp.mean(v, axis=-1, keepdims=True)
    vc = v - mu
    vn = vc * lax.rsqrt(jnp.mean(vc * vc, axis=-1, keepdims=True) + EPS) * lng_ref[...] + lnb_ref[...]
    vn_ref[...] = vn.astype(BF16)
    causal = (lax.broadcasted_iota(jnp.int32, (CHUNK, CHUNK), 0)
              >= lax.broadcasted_iota(jnp.int32, (CHUNK, CHUNK), 1))
    n_chunks = T // CHUNK
    for gi in range(SGU_GROUPS):
        cols = slice(gi * SGU_GROUP_DIM, (gi + 1) * SGU_GROUP_DIM)
        w_sp = jnp.where(causal, wsp_ref[gi], 0.0).astype(BF16)
        rhs = jnp.concatenate([vn_ref[c * CHUNK:(c + 1) * CHUNK, cols] for c in range(n_chunks)], axis=1)
        mixed = _dot(w_sp, rhs)
        for c in range(n_chunks):
            rows = slice(c * CHUNK, (c + 1) * CHUNK)
            blk = mixed[:, c * SGU_GROUP_DIM:(c + 1) * SGU_GROUP_DIM] + bsp_ref[:, cols]
            sg_ref[rows, cols] = (u[rows, cols] * blk).astype(BF16)
    br_b = _dot(sg_ref[...], wb_ref[...])

    c0 = POOL_WIDTH + 2 * SGU_WIDTH
    ga = _dot(h, win_ref[:, c0:c0 + D_MODEL])
    gb = _dot(h, win_ref[:, c0 + D_MODEL:c0 + 2 * D_MODEL])
    merged = jax.nn.sigmoid(ga) * br_a + jax.nn.sigmoid(gb) * br_b
    o_ref[...] = x + _dot(merged.astype(BF16), wout_ref[...])


def _mixer(x, nm, win, pmix, pscale, lng, lnb, wsp, bsp_tile, wa, wb, wout):
    T = TOK_TILE
    tok = pl.BlockSpec((T, D_MODEL), lambda i: (i, 0))
    return pl.pallas_call(
        _mixer_kernel,
        grid=(N_TOKENS // T,),
        in_specs=[
            tok,
            _const_spec((1, D_MODEL)),
            _const_spec((D_MODEL, IN_COLS)),
            _const_spec((POOL_GROUPS, POOL_GROUP_DIM, POOL_GROUP_DIM)),
            _const_spec((1, POOL_WIDTH)),
            _const_spec((1, SGU_WIDTH)),
            _const_spec((1, SGU_WIDTH)),
            _const_spec((SGU_GROUPS, CHUNK, CHUNK)),
            _const_spec((CHUNK, SGU_WIDTH)),
            _const_spec((POOL_WIDTH, D_MODEL)),
            _const_spec((SGU_WIDTH, D_MODEL)),
            _const_spec((D_MODEL, D_MODEL)),
        ],
        out_specs=tok,
        out_shape=jax.ShapeDtypeStruct((N_TOKENS, D_MODEL), F32),
        scratch_shapes=[
            pltpu.VMEM((POOL_HALO + T, POOL_WIDTH), F32),
            pltpu.VMEM((T, POOL_WIDTH), BF16),
            pltpu.VMEM((T, SGU_WIDTH), BF16),
            pltpu.VMEM((T, SGU_WIDTH), BF16),
        ],
        compiler_params=_params(),
        name="mixer",
    )(x, nm, win, pmix, pscale, lng, lnb, wsp, bsp_tile, wa, wb, wout)


def _xattn_kernel(x_ref, g_ref, wq_ref, kv_ref, wo_ref, *rest, route):
    if route:
        gf_ref, wr_ref, o_ref, h_ref, meta_ref, gate_ref, cnt_ref, ocat_ref = rest
    else:
        o_ref, ocat_ref = rest
    x = x_ref[...]
    h = _rms(x, g_ref[...]).astype(BF16)
    q = (_dot(h, wq_ref[...]) * (XATTN_HEAD_DIM ** -0.5)).astype(BF16)
    for hd in range(XATTN_HEADS):
        cols = slice(hd * XATTN_HEAD_DIM, (hd + 1) * XATTN_HEAD_DIM)
        k = kv_ref[:, cols]
        v = kv_ref[:, D_MODEL + hd * XATTN_HEAD_DIM:D_MODEL + (hd + 1) * XATTN_HEAD_DIM]
        s = lax.dot_general(q[:, cols], k, (((1,), (1,)), ((), ())), preferred_element_type=F32)
        e = jnp.exp(s - jnp.max(s, axis=-1, keepdims=True))
        pr = e / jnp.sum(e, axis=-1, keepdims=True)
        ocat_ref[:, cols] = _dot(pr.astype(BF16), v).astype(BF16)
    x = x + _dot(ocat_ref[...], wo_ref[...])
    o_ref[...] = x
    if route:
        _route_rows(x, gf_ref, wr_ref, h_ref, meta_ref, gate_ref, cnt_ref)


def _xattn(x, g, wq, kv_l, wo, router=None):
    T = TOK_TILE
    tok = pl.BlockSpec((T, D_MODEL), lambda i: (i, 0))
    slab = pl.BlockSpec((T, LANES), lambda i: (i, 0))
    in_specs = [
        tok,
        _const_spec((1, D_MODEL)),
        _const_spec((D_MODEL, D_MODEL)),
        pl.BlockSpec((MEM_LEN, 2 * D_MODEL), lambda i: (i // (SEQ // T), 0)),
        _const_spec((D_MODEL, D_MODEL)),
    ]
    out_specs = [tok]
    out_shape = [jax.ShapeDtypeStruct((N_TOKENS, D_MODEL), F32)]
    args = (x, g, wq, kv_l, wo)
    if router is not None:
        in_specs += [_const_spec((1, D_MODEL)), _const_spec((D_MODEL, LANES))]
        out_specs += [tok, slab, slab, pl.BlockSpec((None, 1, LANES), lambda i: (i, 0, 0))]
        out_shape += [
            jax.ShapeDtypeStruct((N_TOKENS, D_MODEL), BF16),
            jax.ShapeDtypeStruct((N_TOKENS, LANES), jnp.int32),
            jax.ShapeDtypeStruct((N_TOKENS, LANES), F32),
            jax.ShapeDtypeStruct((N_BLOCKS, 1, LANES), jnp.int32),
        ]
        args += tuple(router)
    return pl.pallas_call(
        functools.partial(_xattn_kernel, route=router is not None),
        grid=(N_BLOCKS,),
        in_specs=in_specs,
        out_specs=out_specs,
        out_shape=out_shape,
        scratch_shapes=[pltpu.VMEM((T, D_MODEL), BF16)],
        compiler_params=_params(),
        name="xattn",
    )(*args)


def _swiglu_rows(h, wgu_ref, wd_ref, d_ff, n_chunks):
    fc = d_ff // n_chunks
    acc = None
    for c in range(n_chunks):
        g = _dot(h, wgu_ref[:, c * fc:(c + 1) * fc])
        up = _dot(h, wgu_ref[:, d_ff + c * fc:d_ff + (c + 1) * fc])
        a = (g * jax.nn.sigmoid(g) * up).astype(BF16)
        part = _dot(a, wd_ref[c * fc:(c + 1) * fc, :])
        acc = part if acc is None else acc + part
    return acc


def _ffn_kernel(x_ref, g_ref, wgu_ref, wd_ref, o_ref):
    x = x_ref[...]
    h = _rms(x, g_ref[...]).astype(BF16)
    o_ref[...] = x + _swiglu_rows(h, wgu_ref, wd_ref, D_FF, FF_CHUNKS)


def _ffn(x, g, wgu, wd):
    T = TOK_TILE
    tok = pl.BlockSpec((T, D_MODEL), lambda i: (i, 0))
    return pl.pallas_call(
        _ffn_kernel,
        grid=(N_TOKENS // T,),
        in_specs=[tok, _const_spec((1, D_MODEL)), _const_spec((D_MODEL, 2 * D_FF)),
                  _const_spec((D_FF, D_MODEL))],
        out_specs=tok,
        out_shape=jax.ShapeDtypeStruct((N_TOKENS, D_MODEL), F32),
        compiler_params=_params(),
        name="ffn",
    )(x, g, wgu, wd)


def _route_rows(x, g_ref, wr_ref, h_ref, meta_ref, gate_ref, cnt_ref):
    T = TOK_TILE
    h = _rms(x, g_ref[...]).astype(BF16)
    h_ref[...] = h
    lane = lax.broadcasted_iota(jnp.int32, (T, LANES), 1)
    logits = jnp.where(lane < N_EXPERTS, _dot(h, wr_ref[...]), -jnp.inf)
    m1 = jnp.max(logits, axis=-1, keepdims=True)
    i1 = jnp.min(jnp.where(logits == m1, lane, LANES), axis=-1, keepdims=True)
    rest = jnp.where(lane == i1, -jnp.inf, logits)
    m2 = jnp.max(rest, axis=-1, keepdims=True)
    i2 = jnp.min(jnp.where(rest == m2, lane, LANES), axis=-1, keepdims=True)
    e2 = jnp.exp(m2 - m1)
    g1 = 1.0 / (1.0 + e2)
    g2 = e2 / (1.0 + e2)

    sel1 = lane == i1
    sel2 = lane == i2
    onehot = jnp.where(sel1, 1.0, jnp.where(sel2, 1.0, 0.0))
    earlier = (lax.broadcasted_iota(jnp.int32, (T, T), 1)
               < lax.broadcasted_iota(jnp.int32, (T, T), 0))
    before = _dot(jnp.where(earlier, 1.0, 0.0).astype(BF16), onehot.astype(BF16))
    r1 = jnp.sum(jnp.where(sel1, before, 0.0), axis=-1, keepdims=True).astype(jnp.int32)
    r2 = jnp.sum(jnp.where(sel2, before, 0.0), axis=-1, keepdims=True).astype(jnp.int32)

    meta_ref[...] = jnp.where(lane == 0, i1, jnp.where(lane == 1, i2, jnp.where(lane == 2, r1, r2)))
    gate_ref[...] = jnp.where(lane == 0, g1, g2)
    cnt_ref[...] = jnp.sum(onehot, axis=0, keepdims=True).astype(jnp.int32)


def _segment_copies(src, src_row, dst, dst_row, n_rows, max_rows, sem, action):
    size = max_rows
    while size >= SUBLANES:
        done = pl.multiple_of((n_rows // (2 * size)) * (2 * size), SUBLANES)

        @pl.when((n_rows & size) != 0)
        def _(size=size, done=done):
            src_at = 0 if src_row is None else src_row + done
            copy = pltpu.make_async_copy(src.at[pl.ds(src_at, size)],
                                         dst.at[pl.ds(dst_row + done, size)], sem)
            copy.start() if action == "start" else copy.wait()

        size //= 2


def _block_positions(meta_ref, lstart_ref):
    lane = lax.broadcasted_iota(jnp.int32, (TOK_TILE, LANES), 1)
    meta = meta_ref[...]
    lstart = lstart_ref[...]
    pos = []
    for k in range(TOP_K):
        start = jnp.sum(jnp.where(lane == meta[:, k:k + 1], lstart, 0), axis=-1, keepdims=True)
        pos.append(start + meta[:, TOP_K + k:TOP_K + k + 1])
    return pos


def _dispatch_kernel(lstart_s, goff_s, pcnt_s, tail_off_s, tail_len_s, meta_ref, lstart_ref, h_ref,
                     xs_ref, blk_ref, zero_ref, sem, fill_sem):
    b = pl.program_id(0)

    @pl.when(b == 0)
    def _():
        zero_ref[...] = jnp.zeros(zero_ref.shape, F32)
        fill_rows = zero_ref.shape[0]

        def unused_tiles(action):
            def body(i, carry):
                copy = pltpu.make_async_copy(
                    zero_ref, xs_ref.at[pl.ds(pl.multiple_of(i * fill_rows, SUBLANES), fill_rows)],
                    fill_sem)
                copy.start() if action == "start" else copy.wait()
                return carry
            lax.fori_loop(tail_off_s[N_EXPERTS] * (ROW_TILE // fill_rows),
                          N_ROW_TILES * (ROW_TILE // fill_rows), body, 0)

        for action in ("start", "wait"):
            for e in range(N_EXPERTS):
                _segment_copies(zero_ref, None, xs_ref, pl.multiple_of(tail_off_s[e], SUBLANES),
                                tail_len_s[e], fill_rows, fill_sem, action)
            unused_tiles(action)

    def segments(blk, slot, action):
        for e in range(N_EXPERTS):
            _segment_copies(blk_ref.at[slot], pl.multiple_of(lstart_s[blk * N_EXPERTS + e], SUBLANES),
                            xs_ref, pl.multiple_of(goff_s[blk * N_EXPERTS + e], SUBLANES),
                            pcnt_s[blk * N_EXPERTS + e], TOK_TILE, sem.at[slot], action)

    slot = b % 2
    p0, p1 = _block_positions(meta_ref, lstart_ref)
    r = lax.broadcasted_iota(jnp.int32, (TOK_TILE, BLOCK_ROWS), 1)
    sel_t = jnp.where(r == p0, 1.0, jnp.where(r == p1, 1.0, 0.0)).astype(BF16)
    blk_ref[slot] = lax.dot_general(sel_t, h_ref[...], (((0,), (0,)), ((), ())),
                                    preferred_element_type=F32)
    segments(b, slot, "start")

    @pl.when(b > 0)
    def _():
        segments(b - 1, 1 - slot, "wait")

    @pl.when(b == N_BLOCKS - 1)
    def _():
        segments(b, slot, "wait")


def _dispatch(lstart, goff, pcnt, tail_off, tail_len, meta, lstart_rows, h):
    T = TOK_TILE
    grid_spec = pltpu.PrefetchScalarGridSpec(
        num_scalar_prefetch=5,
        grid=(N_BLOCKS,),
        in_specs=[
            pl.BlockSpec((T, LANES), lambda i, *_: (i, 0)),
            pl.BlockSpec((None, 1, LANES), lambda i, *_: (i, 0, 0)),
            pl.BlockSpec((T, D_MODEL), lambda i, *_: (i, 0)),
        ],
        out_specs=pl.BlockSpec(memory_space=pl.ANY),
        scratch_shapes=[pltpu.VMEM((2, BLOCK_ROWS, D_MODEL), F32),
                        pltpu.VMEM((ROW_TILE // 2, D_MODEL), F32),
                        pltpu.SemaphoreType.DMA((2,)),
                        pltpu.SemaphoreType.DMA(())],
    )
    return pl.pallas_call(
        _dispatch_kernel,
        grid_spec=grid_spec,
        out_shape=jax.ShapeDtypeStruct((N_SORTED_ROWS, D_MODEL), F32),
        compiler_params=_params(),
        name="dispatch",
    )(lstart, goff, pcnt, tail_off, tail_len, meta, lstart_rows, h)


def _experts_kernel(tile_e_ref, tile_src_ref, xs_ref, wgu_ref, wd_ref, y_ref):
    del tile_e_ref
    on = tile_src_ref[pl.program_id(0)] == pl.program_id(0)

    @pl.when(on)
    def _():
        y_ref[...] = _swiglu_rows(xs_ref[...].astype(BF16), wgu_ref, wd_ref, D_FF_EXPERT,
                                  EXPERT_FF_CHUNKS)

    @pl.when(jnp.logical_not(on))
    def _():
        y_ref[...] = jnp.zeros((ROW_TILE, D_MODEL), F32)


def _experts(tile_e, tile_src, xs, wgu, wd):
    grid_spec = pltpu.PrefetchScalarGridSpec(
        num_scalar_prefetch=2,
        grid=(N_ROW_TILES,),
        in_specs=[
            pl.BlockSpec((ROW_TILE, D_MODEL), lambda i, te, src: (src[i], 0)),
            pl.BlockSpec((None, D_MODEL, 2 * D_FF_EXPERT), lambda i, te, src: (te[i], 0, 0),
                         pipeline_mode=pl.Buffered(1)),
            pl.BlockSpec((None, D_FF_EXPERT, D_MODEL), lambda i, te, src: (te[i], 0, 0),
                         pipeline_mode=pl.Buffered(1)),
        ],
        out_specs=pl.BlockSpec((ROW_TILE, D_MODEL), lambda i, te, src: (i, 0)),
    )
    return pl.pallas_call(
        _experts_kernel,
        grid_spec=grid_spec,
        out_shape=jax.ShapeDtypeStruct((N_SORTED_ROWS, D_MODEL), F32),
        compiler_params=_params(),
        name="experts",
    )(tile_e, tile_src, xs, wgu, wd)


def _combine_kernel(lstart_s, goff_s, pcnt_s, meta_ref, lstart_ref, x_ref, gate_ref, nf_ref, y_ref,
                    o_ref, buf_ref, sem):
    b = pl.program_id(0)
    slot = b % 2

    def fetch(blk, slot, action):
        for e in range(N_EXPERTS):
            _segment_copies(y_ref, pl.multiple_of(goff_s[blk * N_EXPERTS + e], SUBLANES),
                            buf_ref.at[slot], pl.multiple_of(lstart_s[blk * N_EXPERTS + e], SUBLANES),
                            pcnt_s[blk * N_EXPERTS + e], TOK_TILE, sem.at[slot], action)

    def prefetch(blk, slot):
        buf_ref[slot] = jnp.zeros(buf_ref.shape[1:], F32)
        fetch(blk, slot, "start")

    @pl.when(b == 0)
    def _():
        prefetch(0, 0)

    @pl.when(b + 1 < N_BLOCKS)
    def _():
        prefetch(b + 1, 1 - slot)

    fetch(b, slot, "wait")

    p0, p1 = _block_positions(meta_ref, lstart_ref)
    r = lax.broadcasted_iota(jnp.int32, (TOK_TILE, BLOCK_ROWS), 1)
    gates = gate_ref[...]
    y = buf_ref[slot].astype(BF16)
    pick0 = jnp.where(r == p0, 1.0, 0.0).astype(BF16)
    pick1 = jnp.where(r == p1, 1.0, 0.0).astype(BF16)
    moe = gates[:, 0:1] * _dot(pick0, y) + gates[:, 1:2] * _dot(pick1, y)
    o_ref[...] = _rms(x_ref[...] + moe, nf_ref[...])


def _combine(lstart, goff, pcnt, meta, lstart_rows, x, gates, norm_final, y):
    T = TOK_TILE
    grid_spec = pltpu.PrefetchScalarGridSpec(
        num_scalar_prefetch=3,
        grid=(N_BLOCKS,),
        in_specs=[
            pl.BlockSpec((T, LANES), lambda i, *_: (i, 0)),
            pl.BlockSpec((None, 1, LANES), lambda i, *_: (i, 0, 0)),
            pl.BlockSpec((T, D_MODEL), lambda i, *_: (i, 0)),
            pl.BlockSpec((T, LANES), lambda i, *_: (i, 0)),
            pl.BlockSpec((1, D_MODEL), lambda i, *_: (0, 0)),
            pl.BlockSpec(memory_space=pl.ANY),
        ],
        out_specs=pl.BlockSpec((T, D_MODEL), lambda i, *_: (i, 0)),
        scratch_shapes=[pltpu.VMEM((2, BLOCK_ROWS, D_MODEL), F32), pltpu.SemaphoreType.DMA((2,))],
    )
    return pl.pallas_call(
        _combine_kernel,
        grid_spec=grid_spec,
        out_shape=jax.ShapeDtypeStruct((N_TOKENS, D_MODEL), F32),
        compiler_params=_params(),
        name="combine",
    )(lstart, goff, pcnt, meta, lstart_rows, x, gates, norm_final, y)


def _moe_layer(x, h, meta, gates, counts, w_gu, w_down, norm_final):
    i32 = lambda a: a.astype(jnp.int32)
    pcnt = -(-counts[:, 0, :N_EXPERTS] // SUBLANES) * SUBLANES
    lstart = jnp.cumsum(pcnt, axis=1) - pcnt
    rows_e = jnp.sum(pcnt, axis=0)
    tiles_e = -(-rows_e // ROW_TILE)
    tile_end = jnp.cumsum(tiles_e)
    region = (tile_end - tiles_e) * ROW_TILE
    goff = region[None, :] + jnp.cumsum(pcnt, axis=0) - pcnt
    tail_off = jnp.concatenate([region + rows_e, tile_end[-1:]])
    tail_len = tiles_e * ROW_TILE - rows_e
    tile_ids = jnp.arange(N_ROW_TILES, dtype=jnp.int32)
    tile_src = jnp.minimum(tile_ids, jnp.maximum(tile_end[-1] - 1, 0))
    tile_e = jnp.minimum(jnp.sum(tile_src[:, None] >= tile_end[None, :], axis=1), N_EXPERTS - 1)
    lstart_rows = jnp.pad(lstart, ((0, 0), (0, LANES - N_EXPERTS)))[:, None, :]
    flat = lambda a: i32(a).reshape(-1)

    xs = _dispatch(flat(lstart), flat(goff), flat(pcnt), i32(tail_off), i32(tail_len), meta,
                   i32(lstart_rows), h)
    y = _experts(i32(tile_e), i32(tile_src), xs, w_gu, w_down)
    return _combine(flat(lstart), flat(goff), flat(pcnt), meta, i32(lstart_rows), x, gates,
                    norm_final, y)


def kernel(x, mem, norm_mem, norm_mix, w_in, pool_mix, pool_scale, sgu_ln_g, sgu_ln_b, w_spatial, b_spatial, w_branch_a, w_branch_b, w_out, norm_xattn, w_xq, w_xkv, w_xo, norm_ffn, w_ff_gu, w_ff_down, w_router, w_moe_gu, w_moe_down, norm_final):
    assert DEPTH == 2 and x.shape == (BATCH, SEQ, D_MODEL)
    row = lambda a: a.reshape(1, -1)
    bf = lambda a: a.astype(BF16)

    xt = x.reshape(N_TOKENS, D_MODEL)
    kv = _kv(mem.reshape(BATCH * MEM_LEN, D_MODEL), row(norm_mem), bf(w_xkv))
    for l in range(DEPTH):
        bsp_tile = jnp.repeat(b_spatial[l].T, SGU_GROUP_DIM, axis=1)
        xt = _mixer(xt, row(norm_mix[l]), bf(w_in[l]), bf(pool_mix[l]), row(pool_scale[l]),
                    row(sgu_ln_g[l]), row(sgu_ln_b[l]), w_spatial[l], bsp_tile,
                    bf(w_branch_a[l]), bf(w_branch_b[l]), bf(w_out[l]))
        attn = (xt, row(norm_xattn[l]), bf(w_xq[l]), kv[l], bf(w_xo[l]))
        if l % 2 == 0:
            (xt,) = _xattn(*attn)
            xt = _ffn(xt, row(norm_ffn[l]), bf(w_ff_gu[l // 2]), bf(w_ff_down[l // 2]))
        else:
            wr_pad = bf(jnp.pad(w_router[l // 2], ((0, 0), (0, LANES - N_EXPERTS))))
            xt, h, meta, gates, counts = _xattn(*attn, router=(row(norm_ffn[l]), wr_pad))
            xt = _moe_layer(xt, h, meta, gates, counts, bf(w_moe_gu[l // 2]),
                            bf(w_moe_down[l // 2]), row(norm_final))
    return xt.reshape(BATCH, SEQ, D_MODEL)
```

```python
import functools

import jax
import jax.numpy as jnp
from jax import lax
from jax.experimental import pallas as pl
from jax.experimental.pallas import tpu as pltpu

D_MODEL = 1024
BATCH = 8
SEQ = 4096
DEPTH = 2
MEM_LEN = 256
POOL_WIDTH = 512
POOL_GROUPS = 4
POOL_WINDOWS = (2, 4, 8, 16)
POOL_GROUP_DIM = POOL_WIDTH // POOL_GROUPS
SGU_WIDTH = 512
SGU_GROUPS = 4
SGU_GROUP_DIM = SGU_WIDTH // SGU_GROUPS
CHUNK = 128
IN_COLS = POOL_WIDTH + 2 * SGU_WIDTH + 2 * D_MODEL
XATTN_HEADS = 4
XATTN_HEAD_DIM = D_MODEL // XATTN_HEADS
D_FF = 2816
N_EXPERTS = 8
TOP_K = 2
D_FF_EXPERT = 3584
EPS = 1e-6

N_TOKENS = BATCH * SEQ
LANES = 128
POOL_HALO = 16
TOK_TILE = 512
MIXER_SLAB = 512
ROW_TILE = 512
SUBLANES = 8
N_BLOCKS = N_TOKENS // TOK_TILE
BLOCK_ROWS = -(-(TOK_TILE * TOP_K + N_EXPERTS * (SUBLANES - 1)) // LANES) * LANES
MAX_SORTED = N_TOKENS * TOP_K + N_BLOCKS * N_EXPERTS * (SUBLANES - 1)
N_ROW_TILES = -(-MAX_SORTED // ROW_TILE) + N_EXPERTS
N_SORTED_ROWS = N_ROW_TILES * ROW_TILE
FF_CHUNKS = 2
EXPERT_FF_CHUNKS = 2
VMEM_LIMIT = 52 * 1024 * 1024

BF16 = jnp.bfloat16
F32 = jnp.float32


def _dot(a, b):
    return jnp.dot(a, b, preferred_element_type=F32)


def _rms(x, g):
    return x * lax.rsqrt(jnp.mean(x * x, axis=-1, keepdims=True) + EPS) * g


def _const_spec(shape):
    return pl.BlockSpec(shape, lambda *_: (0,) * len(shape), pipeline_mode=pl.Buffered(1))


def _params(n_axes=1):
    return pltpu.CompilerParams(dimension_semantics=("arbitrary",) * n_axes,
                                vmem_limit_bytes=VMEM_LIMIT)


def _kv_kernel(mem_ref, g_ref, w_ref, o_ref):
    mem_n = _rms(mem_ref[...], g_ref[...]).astype(BF16)
    o_ref[...] = _dot(mem_n, w_ref[...]).astype(BF16)


def _kv(mem2d, norm_mem, w_xkv):
    return pl.pallas_call(
        _kv_kernel,
        grid=(DEPTH, BATCH),
        in_specs=[
            pl.BlockSpec((MEM_LEN, D_MODEL), lambda l, b: (b, 0)),
            pl.BlockSpec((1, D_MODEL), lambda l, b: (0, 0)),
            pl.BlockSpec((None, D_MODEL, 2 * D_MODEL), lambda l, b: (l, 0, 0)),
        ],
        out_specs=pl.BlockSpec((None, MEM_LEN, 2 * D_MODEL), lambda l, b: (l, b, 0)),
        out_shape=jax.ShapeDtypeStruct((DEPTH, BATCH * MEM_LEN, 2 * D_MODEL), BF16),
        compiler_params=_params(2),
        name="kv",
    )(mem2d, norm_mem, w_xkv)


def _mixer_kernel(x_ref, nm_ref, win_ref, pmix_ref, pscale_ref, lng_ref, lnb_ref, wsp_ref,
                  bsp_ref, wa_ref, wb_ref, wout_ref, o_ref, pext_ref, ain_ref, vn_ref, sg_ref):
    T = TOK_TILE
    R = MIXER_SLAB
    G = POOL_GROUP_DIM
    tile_in_seq = pl.program_id(0) % (SEQ // T)

    @pl.when(tile_in_seq == 0)
    def _():
        pext_ref[0:POOL_HALO, :] = jnp.zeros((POOL_HALO, POOL_WIDTH), F32)

    causal = (lax.broadcasted_iota(jnp.int32, (CHUNK, CHUNK), 0)
              >= lax.broadcasted_iota(jnp.int32, (CHUNK, CHUNK), 1))
    w_sp = [jnp.where(causal, wsp_ref[gi], 0.0).astype(BF16) for gi in range(SGU_GROUPS)]

    for r0 in range(0, T, R):
        rows = slice(r0, r0 + R)
        x = x_ref[rows, :]
        h = _rms(x, nm_ref[...]).astype(BF16)

        c0 = POOL_WIDTH + 2 * SGU_WIDTH
        p = _dot(h, win_ref[:, 0:POOL_WIDTH])
        u = _dot(h, win_ref[:, POOL_WIDTH:POOL_WIDTH + SGU_WIDTH])
        v = _dot(h, win_ref[:, POOL_WIDTH + SGU_WIDTH:c0])
        ga = _dot(h, win_ref[:, c0:c0 + D_MODEL])
        gb = _dot(h, win_ref[:, c0 + D_MODEL:c0 + 2 * D_MODEL])
        gate_a = jax.nn.sigmoid(ga)
        gate_b = jax.nn.sigmoid(gb)

        pext_ref[POOL_HALO + r0:POOL_HALO + r0 + R, :] = p
        t_pos = tile_in_seq * T + r0 + lax.broadcasted_iota(jnp.int32, (R, G), 0)
        for gi, w in enumerate(POOL_WINDOWS):
            cols = slice(gi * G, (gi + 1) * G)
            pg = p[:, cols]
            acc = pg
            for j in range(1, w):
                acc = acc + pext_ref[POOL_HALO + r0 - j:POOL_HALO + r0 - j + R, cols]
            cnt = jnp.minimum(t_pos + 1, w).astype(F32)
            pooled = acc / cnt - pg
            mixed = _dot(pooled.astype(BF16), pmix_ref[gi])
            ain_ref[rows, cols] = (mixed * pscale_ref[:, cols]).astype(BF16)
        br_a = _dot(ain_ref[rows, :], wa_ref[...])

        u = jax.nn.gelu(u)
        v = jax.nn.gelu(v)
        mu = jnp.mean(v, axis=-1, keepdims=True)
        vc = v - mu
        vn = (vc * lax.rsqrt(jnp.mean(vc * vc, axis=-1, keepdims=True) + EPS) * lng_ref[...]
              + lnb_ref[...])
        vn_ref[rows, :] = vn.astype(BF16)
        n_chunks = R // CHUNK
        for gi in range(SGU_GROUPS):
            cols = slice(gi * SGU_GROUP_DIM, (gi + 1) * SGU_GROUP_DIM)
            rhs = jnp.concatenate([vn_ref[r0 + c * CHUNK:r0 + (c + 1) * CHUNK, cols]
                                   for c in range(n_chunks)], axis=1)
            mixed = _dot(w_sp[gi], rhs)
            for c in range(n_chunks):
                crow = slice(c * CHUNK, (c + 1) * CHUNK)
                blk = mixed[:, c * SGU_GROUP_DIM:(c + 1) * SGU_GROUP_DIM] + bsp_ref[:, cols]
                sg_ref[r0 + c * CHUNK:r0 + (c + 1) * CHUNK, cols] = (u[crow, cols] * blk).astype(BF16)
        br_b = _dot(sg_ref[rows, :], wb_ref[...])

        merged = gate_a * br_a + gate_b * br_b
        o_ref[rows, :] = x + _dot(merged.astype(BF16), wout_ref[...])

    pext_ref[0:POOL_HALO, :] = pext_ref[T:T + POOL_HALO, :]


def _mixer(x, nm, win, pmix, pscale, lng, lnb, wsp, bsp_tile, wa, wb, wout):
    T = TOK_TILE
    tok = pl.BlockSpec((T, D_MODEL), lambda i: (i, 0))
    return pl.pallas_call(
        _mixer_kernel,
        grid=(N_TOKENS // T,),
        in_specs=[
            tok,
            _const_spec((1, D_MODEL)),
            _const_spec((D_MODEL, IN_COLS)),
            _const_spec((POOL_GROUPS, POOL_GROUP_DIM, POOL_GROUP_DIM)),
            _const_spec((1, POOL_WIDTH)),
            _const_spec((1, SGU_WIDTH)),
            _const_spec((1, SGU_WIDTH)),
            _const_spec((SGU_GROUPS, CHUNK, CHUNK)),
            _const_spec((CHUNK, SGU_WIDTH)),
            _const_spec((POOL_WIDTH, D_MODEL)),
            _const_spec((SGU_WIDTH, D_MODEL)),
            _const_spec((D_MODEL, D_MODEL)),
        ],
        out_specs=tok,
        out_shape=jax.ShapeDtypeStruct((N_TOKENS, D_MODEL), F32),
        scratch_shapes=[
            pltpu.VMEM((POOL_HALO + T, POOL_WIDTH), F32),
            pltpu.VMEM((T, POOL_WIDTH), BF16),
            pltpu.VMEM((T, SGU_WIDTH), BF16),
            pltpu.VMEM((T, SGU_WIDTH), BF16),
        ],
        compiler_params=_params(),
        name="mixer",
    )(x, nm, win, pmix, pscale, lng, lnb, wsp, bsp_tile, wa, wb, wout)


def _xattn_kernel(x_ref, g_ref, wq_ref, kv_ref, wo_ref, *rest, route):
    if route:
        gf_ref, wr_ref, o_ref, h_ref, meta_ref, gate_ref, cnt_ref, ocat_ref = rest
    else:
        o_ref, ocat_ref = rest
    x = x_ref[...]
    h = _rms(x, g_ref[...]).astype(BF16)
    q = (_dot(h, wq_ref[...]) * (XATTN_HEAD_DIM ** -0.5)).astype(BF16)
    head_cols = [slice(hd * XATTN_HEAD_DIM, (hd + 1) * XATTN_HEAD_DIM) for hd in range(XATTN_HEADS)]
    scores = [lax.dot_general(q[:, cols], kv_ref[:, cols], (((1,), (1,)), ((), ())),
                              preferred_element_type=F32) for cols in head_cols]
    for hd, cols in enumerate(head_cols):
        v = kv_ref[:, D_MODEL + hd * XATTN_HEAD_DIM:D_MODEL + (hd + 1) * XATTN_HEAD_DIM]
        s = scores[hd]
        e = jnp.exp(s - jnp.max(s, axis=-1, keepdims=True))
        pr = e / jnp.sum(e, axis=-1, keepdims=True)
        ocat_ref[:, cols] = _dot(pr.astype(BF16), v).astype(BF16)
    x = x + _dot(ocat_ref[...], wo_ref[...])
    o_ref[...] = x
    if route:
        _route_rows(x, gf_ref, wr_ref, h_ref, meta_ref, gate_ref, cnt_ref)


def _xattn(x, g, wq, kv_l, wo, router=None):
    T = TOK_TILE
    tok = pl.BlockSpec((T, D_MODEL), lambda i: (i, 0))
    slab = pl.BlockSpec((T, LANES), lambda i: (i, 0))
    in_specs = [
        tok,
        _const_spec((1, D_MODEL)),
        _const_spec((D_MODEL, D_MODEL)),
        pl.BlockSpec((MEM_LEN, 2 * D_MODEL), lambda i: (i // (SEQ // T), 0)),
        _const_spec((D_MODEL, D_MODEL)),
    ]
    out_specs = [tok]
    out_shape = [jax.ShapeDtypeStruct((N_TOKENS, D_MODEL), F32)]
    args = (x, g, wq, kv_l, wo)
    if router is not None:
        in_specs += [_const_spec((1, D_MODEL)), _const_spec((D_MODEL, LANES))]
        out_specs += [tok, slab, slab, pl.BlockSpec((None, 1, LANES), lambda i: (i, 0, 0))]
        out_shape += [
            jax.ShapeDtypeStruct((N_TOKENS, D_MODEL), BF16),
            jax.ShapeDtypeStruct((N_TOKENS, LANES), jnp.int32),
            jax.ShapeDtypeStruct((N_TOKENS, LANES), F32),
            jax.ShapeDtypeStruct((N_BLOCKS, 1, LANES), jnp.int32),
        ]
        args += tuple(router)
    return pl.pallas_call(
        functools.partial(_xattn_kernel, route=router is not None),
        grid=(N_BLOCKS,),
        in_specs=in_specs,
        out_specs=out_specs,
        out_shape=out_shape,
        scratch_shapes=[pltpu.VMEM((T, D_MODEL), BF16)],
        compiler_params=_params(),
        name="xattn",
    )(*args)


def _swiglu_rows(h, wgu_ref, wd_ref, d_ff, n_chunks):
    fc = d_ff // n_chunks
    acc = None
    for c in range(n_chunks):
        g = _dot(h, wgu_ref[:, c * fc:(c + 1) * fc])
        up = _dot(h, wgu_ref[:, d_ff + c * fc:d_ff + (c + 1) * fc])
        a = (g * jax.nn.sigmoid(g) * up).astype(BF16)
        part = _dot(a, wd_ref[c * fc:(c + 1) * fc, :])
        acc = part if acc is None else acc + part
    return acc


def _ffn_kernel(x_ref, g_ref, wgu_ref, wd_ref, o_ref):
    x = x_ref[...]
    h = _rms(x, g_ref[...]).astype(BF16)
    o_ref[...] = x + _swiglu_rows(h, wgu_ref, wd_ref, D_FF, FF_CHUNKS)


def _ffn(x, g, wgu, wd):
    T = TOK_TILE
    tok = pl.BlockSpec((T, D_MODEL), lambda i: (i, 0))
    return pl.pallas_call(
        _ffn_kernel,
        grid=(N_TOKENS // T,),
        in_specs=[tok, _const_spec((1, D_MODEL)), _const_spec((D_MODEL, 2 * D_FF)),
                  _const_spec((D_FF, D_MODEL))],
        out_specs=tok,
        out_shape=jax.ShapeDtypeStruct((N_TOKENS, D_MODEL), F32),
        compiler_params=_params(),
        name="ffn",
    )(x, g, wgu, wd)


def _route_rows(x, g_ref, wr_ref, h_ref, meta_ref, gate_ref, cnt_ref):
    T = TOK_TILE
    h = _rms(x, g_ref[...]).astype(BF16)
    h_ref[...] = h
    lane = lax.broadcasted_iota(jnp.int32, (T, LANES), 1)
    logits = jnp.where(lane < N_EXPERTS, _dot(h, wr_ref[...]), -jnp.inf)
    m1 = jnp.max(logits, axis=-1, keepdims=True)
    i1 = jnp.min(jnp.where(logits == m1, lane, LANES), axis=-1, keepdims=True)
    rest = jnp.where(lane == i1, -jnp.inf, logits)
    m2 = jnp.max(rest, axis=-1, keepdims=True)
    i2 = jnp.min(jnp.where(rest == m2, lane, LANES), axis=-1, keepdims=True)
    e2 = jnp.exp(m2 - m1)
    g1 = 1.0 / (1.0 + e2)
    g2 = e2 / (1.0 + e2)

    sel1 = lane == i1
    sel2 = lane == i2
    onehot = jnp.where(sel1, 1.0, jnp.where(sel2, 1.0, 0.0))
    earlier = (lax.broadcasted_iota(jnp.int32, (T, T), 1)
               < lax.broadcasted_iota(jnp.int32, (T, T), 0))
    before = _dot(jnp.where(earlier, 1.0, 0.0).astype(BF16), onehot.astype(BF16))
    r1 = jnp.sum(jnp.where(sel1, before, 0.0), axis=-1, keepdims=True).astype(jnp.int32)
    r2 = jnp.sum(jnp.where(sel2, before, 0.0), axis=-1, keepdims=True).astype(jnp.int32)

    meta_ref[...] = jnp.where(lane == 0, i1, jnp.where(lane == 1, i2, jnp.where(lane == 2, r1, r2)))
    gate_ref[...] = jnp.where(lane == 0, g1, g2)
    cnt_ref[...] = jnp.sum(onehot, axis=0, keepdims=True).astype(jnp.int32)


def _segment_copies(src, src_row, dst, dst_row, n_rows, max_rows, sem, action):
    size = max_rows
    while size >= SUBLANES:
        done = pl.multiple_of((n_rows // (2 * size)) * (2 * size), SUBLANES)

        @pl.when((n_rows & size) != 0)
        def _(size=size, done=done):
            src_at = 0 if src_row is None else src_row + done
            copy = pltpu.make_async_copy(src.at[pl.ds(src_at, size)],
                                         dst.at[pl.ds(dst_row + done, size)], sem)
            copy.start() if action == "start" else copy.wait()

        size //= 2


def _block_positions(meta_ref, lstart_ref):
    lane = lax.broadcasted_iota(jnp.int32, (TOK_TILE, LANES), 1)
    meta = meta_ref[...]
    lstart = lstart_ref[...]
    pos = []
    for k in range(TOP_K):
        start = jnp.sum(jnp.where(lane == meta[:, k:k + 1], lstart, 0), axis=-1, keepdims=True)
        pos.append(start + meta[:, TOP_K + k:TOP_K + k + 1])
    return pos


def _dispatch_kernel(lstart_s, goff_s, pcnt_s, tail_off_s, tail_len_s, meta_ref, lstart_ref, h_ref,
                     xs_ref, blk_ref, zero_ref, sem, fill_sem):
    b = pl.program_id(0)

    @pl.when(b == 0)
    def _():
        zero_ref[...] = jnp.zeros(zero_ref.shape, F32)
        fill_rows = zero_ref.shape[0]

        def unused_tiles(action):
            def body(i, carry):
                copy = pltpu.make_async_copy(
                    zero_ref, xs_ref.at[pl.ds(pl.multiple_of(i * fill_rows, SUBLANES), fill_rows)],
                    fill_sem)
                copy.start() if action == "start" else copy.wait()
                return carry
            lax.fori_loop(tail_off_s[N_EXPERTS] * (ROW_TILE // fill_rows),
                          N_ROW_TILES * (ROW_TILE // fill_rows), body, 0)

        for action in ("start", "wait"):
            for e in range(N_EXPERTS):
                _segment_copies(zero_ref, None, xs_ref, pl.multiple_of(tail_off_s[e], SUBLANES),
                                tail_len_s[e], fill_rows, fill_sem, action)
            unused_tiles(action)

    def segments(blk, slot, action):
        for e in range(N_EXPERTS):
            _segment_copies(blk_ref.at[slot], pl.multiple_of(lstart_s[blk * N_EXPERTS + e], SUBLANES),
                            xs_ref, pl.multiple_of(goff_s[blk * N_EXPERTS + e], SUBLANES),
                            pcnt_s[blk * N_EXPERTS + e], TOK_TILE, sem.at[slot], action)

    slot = b % 2
    p0, p1 = _block_positions(meta_ref, lstart_ref)
    r = lax.broadcasted_iota(jnp.int32, (TOK_TILE, BLOCK_ROWS), 1)
    sel_t = jnp.where(r == p0, 1.0, jnp.where(r == p1, 1.0, 0.0)).astype(BF16)
    blk_ref[slot] = lax.dot_general(sel_t, h_ref[...], (((0,), (0,)), ((), ())),
                                    preferred_element_type=F32)
    segments(b, slot, "start")

    @pl.when(b > 0)
    def _():
        segments(b - 1, 1 - slot, "wait")

    @pl.when(b == N_BLOCKS - 1)
    def _():
        segments(b, slot, "wait")


def _dispatch(lstart, goff, pcnt, tail_off, tail_len, meta, lstart_rows, h):
    T = TOK_TILE
    grid_spec = pltpu.PrefetchScalarGridSpec(
        num_scalar_prefetch=5,
        grid=(N_BLOCKS,),
        in_specs=[
            pl.BlockSpec((T, LANES), lambda i, *_: (i, 0)),
            pl.BlockSpec((None, 1, LANES), lambda i, *_: (i, 0, 0)),
            pl.BlockSpec((T, D_MODEL), lambda i, *_: (i, 0)),
        ],
        out_specs=pl.BlockSpec(memory_space=pl.ANY),
        scratch_shapes=[pltpu.VMEM((2, BLOCK_ROWS, D_MODEL), F32),
                        pltpu.VMEM((ROW_TILE // 2, D_MODEL), F32),
                        pltpu.SemaphoreType.DMA((2,)),
                        pltpu.SemaphoreType.DMA(())],
    )
    return pl.pallas_call(
        _dispatch_kernel,
        grid_spec=grid_spec,
        out_shape=jax.ShapeDtypeStruct((N_SORTED_ROWS, D_MODEL), F32),
        compiler_params=_params(),
        name="dispatch",
    )(lstart, goff, pcnt, tail_off, tail_len, meta, lstart_rows, h)


def _experts_kernel(tile_e_ref, tile_src_ref, xs_ref, wgu_ref, wd_ref, y_ref):
    del tile_e_ref
    on = tile_src_ref[pl.program_id(0)] == pl.program_id(0)

    @pl.when(on)
    def _():
        y_ref[...] = _swiglu_rows(xs_ref[...].astype(BF16), wgu_ref, wd_ref, D_FF_EXPERT,
                                  EXPERT_FF_CHUNKS)

    @pl.when(jnp.logical_not(on))
    def _():
        y_ref[...] = jnp.zeros((ROW_TILE, D_MODEL), F32)


def _experts(tile_e, tile_src, xs, wgu, wd):
    grid_spec = pltpu.PrefetchScalarGridSpec(
        num_scalar_prefetch=2,
        grid=(N_ROW_TILES,),
        in_specs=[
            pl.BlockSpec((ROW_TILE, D_MODEL), lambda i, te, src: (src[i], 0)),
            pl.BlockSpec((None, D_MODEL, 2 * D_FF_EXPERT), lambda i, te, src: (te[i], 0, 0),
                         pipeline_mode=pl.Buffered(1)),
            pl.BlockSpec((None, D_FF_EXPERT, D_MODEL), lambda i, te, src: (te[i], 0, 0),
                         pipeline_mode=pl.Buffered(1)),
        ],
        out_specs=pl.BlockSpec((ROW_TILE, D_MODEL), lambda i, te, src: (i, 0)),
    )
    return pl.pallas_call(
        _experts_kernel,
        grid_spec=grid_spec,
        out_shape=jax.ShapeDtypeStruct((N_SORTED_ROWS, D_MODEL), F32),
        compiler_params=_params(),
        name="experts",
    )(tile_e, tile_src, xs, wgu, wd)


def _combine_kernel(lstart_s, goff_s, pcnt_s, meta_ref, lstart_ref, x_ref, gate_ref, nf_ref, y_ref,
                    o_ref, buf_ref, sem):
    b = pl.program_id(0)
    slot = b % 2

    def fetch(blk, slot, action):
        for e in range(N_EXPERTS):
            _segment_copies(y_ref, pl.multiple_of(goff_s[blk * N_EXPERTS + e], SUBLANES),
                            buf_ref.at[slot], pl.multiple_of(lstart_s[blk * N_EXPERTS + e], SUBLANES),
                            pcnt_s[blk * N_EXPERTS + e], TOK_TILE, sem.at[slot], action)

    def prefetch(blk, slot):
        buf_ref[slot] = jnp.zeros(buf_ref.shape[1:], F32)
        fetch(blk, slot, "start")

    @pl.when(b == 0)
    def _():
        prefetch(0, 0)

    @pl.when(b + 1 < N_BLOCKS)
    def _():
        prefetch(b + 1, 1 - slot)

    fetch(b, slot, "wait")

    p0, p1 = _block_positions(meta_ref, lstart_ref)
    r = lax.broadcasted_iota(jnp.int32, (TOK_TILE, BLOCK_ROWS), 1)
    gates = gate_ref[...]
    y = buf_ref[slot].astype(BF16)
    pick0 = jnp.where(r == p0, 1.0, 0.0).astype(BF16)
    pick1 = jnp.where(r == p1, 1.0, 0.0).astype(BF16)
    moe = gates[:, 0:1] * _dot(pick0, y) + gates[:, 1:2] * _dot(pick1, y)
    o_ref[...] = _rms(x_ref[...] + moe, nf_ref[...])


def _combine(lstart, goff, pcnt, meta, lstart_rows, x, gates, norm_final, y):
    T = TOK_TILE
    grid_spec = pltpu.PrefetchScalarGridSpec(
        num_scalar_prefetch=3,
        grid=(N_BLOCKS,),
        in_specs=[
            pl.BlockSpec((T, LANES), lambda i, *_: (i, 0)),
            pl.BlockSpec((None, 1, LANES), lambda i, *_: (i, 0, 0)),
            pl.BlockSpec((T, D_MODEL), lambda i, *_: (i, 0)),
            pl.BlockSpec((T, LANES), lambda i, *_: (i, 0)),
            pl.BlockSpec((1, D_MODEL), lambda i, *_: (0, 0)),
            pl.BlockSpec(memory_space=pl.ANY),
        ],
        out_specs=pl.BlockSpec((T, D_MODEL), lambda i, *_: (i, 0)),
        scratch_shapes=[pltpu.VMEM((2, BLOCK_ROWS, D_MODEL), F32), pltpu.SemaphoreType.DMA((2,))],
    )
    return pl.pallas_call(
        _combine_kernel,
        grid_spec=grid_spec,
        out_shape=jax.ShapeDtypeStruct((N_TOKENS, D_MODEL), F32),
        compiler_params=_params(),
        name="combine",
    )(lstart, goff, pcnt, meta, lstart_rows, x, gates, norm_final, y)


def _moe_layer(x, h, meta, gates, counts, w_gu, w_down, norm_final):
    i32 = lambda a: a.astype(jnp.int32)
    pcnt = -(-counts[:, 0, :N_EXPERTS] // SUBLANES) * SUBLANES
    lstart = jnp.cumsum(pcnt, axis=1) - pcnt
    rows_e = jnp.sum(pcnt, axis=0)
    tiles_e = -(-rows_e // ROW_TILE)
    tile_end = jnp.cumsum(tiles_e)
    region = (tile_end - tiles_e) * ROW_TILE
    goff = region[None, :] + jnp.cumsum(pcnt, axis=0) - pcnt
    tail_off = jnp.concatenate([region + rows_e, tile_end[-1:]])
    tail_len = tiles_e * ROW_TILE - rows_e
    tile_ids = jnp.arange(N_ROW_TILES, dtype=jnp.int32)
    tile_src = jnp.minimum(tile_ids, jnp.maximum(tile_end[-1] - 1, 0))
    tile_e = jnp.minimum(jnp.sum(tile_src[:, None] >= tile_end[None, :], axis=1), N_EXPERTS - 1)
    lstart_rows = jnp.pad(lstart, ((0, 0), (0, LANES - N_EXPERTS)))[:, None, :]
    flat = lambda a: i32(a).reshape(-1)

    xs = _dispatch(flat(lstart), flat(goff), flat(pcnt), i32(tail_off), i32(tail_len), meta,
                   i32(lstart_rows), h)
    y = _experts(i32(tile_e), i32(tile_src), xs, w_gu, w_down)
    return _combine(flat(lstart), flat(goff), flat(pcnt), meta, i32(lstart_rows), x, gates,
                    norm_final, y)


def kernel(x, mem, norm_mem, norm_mix, w_in, pool_mix, pool_scale, sgu_ln_g, sgu_ln_b, w_spatial, b_spatial, w_branch_a, w_branch_b, w_out, norm_xattn, w_xq, w_xkv, w_xo, norm_ffn, w_ff_gu, w_ff_down, w_router, w_moe_gu, w_moe_down, norm_final):
    assert DEPTH == 2 and x.shape == (BATCH, SEQ, D_MODEL)
    row = lambda a: a.reshape(1, -1)
    bf = lambda a: a.astype(BF16)

    xt = x.reshape(N_TOKENS, D_MODEL)
    kv = _kv(mem.reshape(BATCH * MEM_LEN, D_MODEL), row(norm_mem), bf(w_xkv))
    for l in range(DEPTH):
        bsp_tile = jnp.repeat(b_spatial[l].T, SGU_GROUP_DIM, axis=1)
        xt = _mixer(xt, row(norm_mix[l]), bf(w_in[l]), bf(pool_mix[l]), row(pool_scale[l]),
                    row(sgu_ln_g[l]), row(sgu_ln_b[l]), w_spatial[l], bsp_tile,
                    bf(w_branch_a[l]), bf(w_branch_b[l]), bf(w_out[l]))
        attn = (xt, row(norm_xattn[l]), bf(w_xq[l]), kv[l], bf(w_xo[l]))
        if l % 2 == 0:
            (xt,) = _xattn(*attn)
            xt = _ffn(xt, row(norm_ffn[l]), bf(w_ff_gu[l // 2]), bf(w_ff_down[l // 2]))
        else:
            wr_pad = bf(jnp.pad(w_router[l // 2], ((0, 0), (0, LANES - N_EXPERTS))))
            xt, h, meta, gates, counts = _xattn(*attn, router=(row(norm_ffn[l]), wr_pad))
            xt = _moe_layer(xt, h, meta, gates, counts, bf(w_moe_gu[l // 2]),
                            bf(w_moe_down[l // 2]), row(norm_final))
    return xt.reshape(BATCH, SEQ, D_MODEL)
```

```python
import functools

import jax
import jax.numpy as jnp
from jax import lax
from jax.experimental import pallas as pl
from jax.experimental.pallas import tpu as pltpu

D_MODEL = 1024
BATCH = 8
SEQ = 4096
DEPTH = 2
MEM_LEN = 256
POOL_WIDTH = 512
POOL_GROUPS = 4
POOL_WINDOWS = (2, 4, 8, 16)
POOL_GROUP_DIM = POOL_WIDTH // POOL_GROUPS
SGU_WIDTH = 512
SGU_GROUPS = 4
SGU_GROUP_DIM = SGU_WIDTH // SGU_GROUPS
CHUNK = 128
IN_COLS = POOL_WIDTH + 2 * SGU_WIDTH + 2 * D_MODEL
XATTN_HEADS = 4
XATTN_HEAD_DIM = D_MODEL // XATTN_HEADS
D_FF = 2816
N_EXPERTS = 8
TOP_K = 2
D_FF_EXPERT = 3584
EPS = 1e-6

N_TOKENS = BATCH * SEQ
LANES = 128
POOL_HALO = 16
TOK_TILE = 512
MIXER_SLAB = 512
ROW_TILE = 512
SUBLANES = 8
N_BLOCKS = N_TOKENS // TOK_TILE
BLOCK_ROWS = -(-(TOK_TILE * TOP_K + N_EXPERTS * (SUBLANES - 1)) // LANES) * LANES
MAX_SORTED = N_TOKENS * TOP_K + N_BLOCKS * N_EXPERTS * (SUBLANES - 1)
N_ROW_TILES = -(-MAX_SORTED // ROW_TILE) + N_EXPERTS
N_SORTED_ROWS = N_ROW_TILES * ROW_TILE
FF_CHUNKS = 2
EXPERT_FF_CHUNKS = 2
VMEM_LIMIT = 52 * 1024 * 1024

BF16 = jnp.bfloat16
F32 = jnp.float32


def _dot(a, b):
    return jnp.dot(a, b, preferred_element_type=F32)


def _rms(x, g):
    return x * lax.rsqrt(jnp.mean(x * x, axis=-1, keepdims=True) + EPS) * g


def _const_spec(shape):
    return pl.BlockSpec(shape, lambda *_: (0,) * len(shape), pipeline_mode=pl.Buffered(1))


def _params(n_axes=1):
    return pltpu.CompilerParams(dimension_semantics=("arbitrary",) * n_axes,
                                vmem_limit_bytes=VMEM_LIMIT)


def _kv_kernel(mem_ref, g_ref, w_ref, o_ref):
    mem_n = _rms(mem_ref[...], g_ref[...]).astype(BF16)
    o_ref[...] = _dot(mem_n, w_ref[...]).astype(BF16)


def _kv(mem2d, norm_mem, w_xkv):
    return pl.pallas_call(
        _kv_kernel,
        grid=(DEPTH, BATCH),
        in_specs=[
            pl.BlockSpec((MEM_LEN, D_MODEL), lambda l, b: (b, 0)),
            pl.BlockSpec((1, D_MODEL), lambda l, b: (0, 0)),
            pl.BlockSpec((None, D_MODEL, 2 * D_MODEL), lambda l, b: (l, 0, 0)),
        ],
        out_specs=pl.BlockSpec((None, MEM_LEN, 2 * D_MODEL), lambda l, b: (l, b, 0)),
        out_shape=jax.ShapeDtypeStruct((DEPTH, BATCH * MEM_LEN, 2 * D_MODEL), BF16),
        compiler_params=_params(2),
        name="kv",
    )(mem2d, norm_mem, w_xkv)


def _mixer_kernel(x_ref, nm_ref, win_ref, pmix_ref, pscale_ref, lng_ref, lnb_ref, wsp_ref,
                  bsp_ref, wa_ref, wb_ref, wout_ref, o_ref, pext_ref, ain_ref, vn_ref, sg_ref):
    T = TOK_TILE
    R = MIXER_SLAB
    G = POOL_GROUP_DIM
    tile_in_seq = pl.program_id(0) % (SEQ // T)

    @pl.when(tile_in_seq == 0)
    def _():
        pext_ref[0:POOL_HALO, :] = jnp.zeros((POOL_HALO, POOL_WIDTH), F32)

    causal = (lax.broadcasted_iota(jnp.int32, (CHUNK, CHUNK), 0)
              >= lax.broadcasted_iota(jnp.int32, (CHUNK, CHUNK), 1))
    w_sp = [jnp.where(causal, wsp_ref[gi], 0.0).astype(BF16) for gi in range(SGU_GROUPS)]

    for r0 in range(0, T, R):
        rows = slice(r0, r0 + R)
        x = x_ref[rows, :]
        h = _rms(x, nm_ref[...]).astype(BF16)

        c0 = POOL_WIDTH + 2 * SGU_WIDTH
        p = _dot(h, win_ref[:, 0:POOL_WIDTH])
        u = _dot(h, win_ref[:, POOL_WIDTH:POOL_WIDTH + SGU_WIDTH])
        v = _dot(h, win_ref[:, POOL_WIDTH + SGU_WIDTH:c0])
        ga = _dot(h, win_ref[:, c0:c0 + D_MODEL])
        gb = _dot(h, win_ref[:, c0 + D_MODEL:c0 + 2 * D_MODEL])
        gate_a = jax.nn.sigmoid(ga)
        gate_b = jax.nn.sigmoid(gb)

        pext_ref[POOL_HALO + r0:POOL_HALO + r0 + R, :] = p
        t_pos = tile_in_seq * T + r0 + lax.broadcasted_iota(jnp.int32, (R, G), 0)
        for gi, w in enumerate(POOL_WINDOWS):
            cols = slice(gi * G, (gi + 1) * G)
            pg = p[:, cols]
            acc = pg
            for j in range(1, w):
                acc = acc + pext_ref[POOL_HALO + r0 - j:POOL_HALO + r0 - j + R, cols]
            cnt = jnp.minimum(t_pos + 1, w).astype(F32)
            pooled = acc / cnt - pg
            mixed = _dot(pooled.astype(BF16), pmix_ref[gi])
            ain_ref[rows, cols] = (mixed * pscale_ref[:, cols]).astype(BF16)
        br_a = _dot(ain_ref[rows, :], wa_ref[...])

        u = jax.nn.gelu(u)
        v = jax.nn.gelu(v)
        mu = jnp.mean(v, axis=-1, keepdims=True)
        vc = v - mu
        vn = (vc * lax.rsqrt(jnp.mean(vc * vc, axis=-1, keepdims=True) + EPS) * lng_ref[...]
              + lnb_ref[...])
        vn_ref[rows, :] = vn.astype(BF16)
        n_chunks = R // CHUNK
        for gi in range(SGU_GROUPS):
            cols = slice(gi * SGU_GROUP_DIM, (gi + 1) * SGU_GROUP_DIM)
            rhs = jnp.concatenate([vn_ref[r0 + c * CHUNK:r0 + (c + 1) * CHUNK, cols]
                                   for c in range(n_chunks)], axis=1)
            mixed = _dot(w_sp[gi], rhs)
            for c in range(n_chunks):
                crow = slice(c * CHUNK, (c + 1) * CHUNK)
                blk = mixed[:, c * SGU_GROUP_DIM:(c + 1) * SGU_GROUP_DIM] + bsp_ref[:, cols]
                sg_ref[r0 + c * CHUNK:r0 + (c + 1) * CHUNK, cols] = (u[crow, cols] * blk).astype(BF16)
        br_b = _dot(sg_ref[rows, :], wb_ref[...])

        merged = gate_a * br_a + gate_b * br_b
        o_ref[rows, :] = x + _dot(merged.astype(BF16), wout_ref[...])

    pext_ref[0:POOL_HALO, :] = pext_ref[T:T + POOL_HALO, :]


def _mixer(x, nm, win, pmix, pscale, lng, lnb, wsp, bsp_tile, wa, wb, wout):
    T = TOK_TILE
    tok = pl.BlockSpec((T, D_MODEL), lambda i: (i, 0))
    return pl.pallas_call(
        _mixer_kernel,
        grid=(N_TOKENS // T,),
        in_specs=[
            tok,
            _const_spec((1, D_MODEL)),
            _const_spec((D_MODEL, IN_COLS)),
            _const_spec((POOL_GROUPS, POOL_GROUP_DIM, POOL_GROUP_DIM)),
            _const_spec((1, POOL_WIDTH)),
            _const_spec((1, SGU_WIDTH)),
            _const_spec((1, SGU_WIDTH)),
            _const_spec((SGU_GROUPS, CHUNK, CHUNK)),
            _const_spec((CHUNK, SGU_WIDTH)),
            _const_spec((POOL_WIDTH, D_MODEL)),
            _const_spec((SGU_WIDTH, D_MODEL)),
            _const_spec((D_MODEL, D_MODEL)),
        ],
        out_specs=tok,
        out_shape=jax.ShapeDtypeStruct((N_TOKENS, D_MODEL), F32),
        scratch_shapes=[
            pltpu.VMEM((POOL_HALO + T, POOL_WIDTH), F32),
            pltpu.VMEM((T, POOL_WIDTH), BF16),
            pltpu.VMEM((T, SGU_WIDTH), BF16),
            pltpu.VMEM((T, SGU_WIDTH), BF16),
        ],
        compiler_params=_params(),
        name="mixer",
    )(x, nm, win, pmix, pscale, lng, lnb, wsp, bsp_tile, wa, wb, wout)


def _xattn_kernel(x_ref, g_ref, wq_ref, kv_ref, wo_ref, *rest, route):
    if route:
        gf_ref, wr_ref, o_ref, h_ref, meta_ref, gate_ref, cnt_ref, ocat_ref = rest
    else:
        o_ref, ocat_ref = rest
    x = x_ref[...]
    h = _rms(x, g_ref[...]).astype(BF16)
    q = (_dot(h, wq_ref[...]) * (XATTN_HEAD_DIM ** -0.5)).astype(BF16)
    head_cols = [slice(hd * XATTN_HEAD_DIM, (hd + 1) * XATTN_HEAD_DIM) for hd in range(XATTN_HEADS)]
    scores = [lax.dot_general(q[:, cols], kv_ref[:, cols], (((1,), (1,)), ((), ())),
                              preferred_element_type=F32) for cols in head_cols]
    for hd, cols in enumerate(head_cols):
        v = kv_ref[:, D_MODEL + hd * XATTN_HEAD_DIM:D_MODEL + (hd + 1) * XATTN_HEAD_DIM]
        s = scores[hd]
        e = jnp.exp(s - jnp.max(s, axis=-1, keepdims=True))
        pr = e / jnp.sum(e, axis=-1, keepdims=True)
        ocat_ref[:, cols] = _dot(pr.astype(BF16), v).astype(BF16)
    x = x + _dot(ocat_ref[...], wo_ref[...])
    o_ref[...] = x
    if route:
        _route_rows(x, gf_ref, wr_ref, h_ref, meta_ref, gate_ref, cnt_ref)


def _xattn(x, g, wq, kv_l, wo, router=None):
    T = TOK_TILE
    tok = pl.BlockSpec((T, D_MODEL), lambda i: (i, 0))
    slab = pl.BlockSpec((T, LANES), lambda i: (i, 0))
    in_specs = [
        tok,
        _const_spec((1, D_MODEL)),
        _const_spec((D_MODEL, D_MODEL)),
        pl.BlockSpec((MEM_LEN, 2 * D_MODEL), lambda i: (i // (SEQ // T), 0)),
        _const_spec((D_MODEL, D_MODEL)),
    ]
    out_specs = [tok]
    out_shape = [jax.ShapeDtypeStruct((N_TOKENS, D_MODEL), F32)]
    args = (x, g, wq, kv_l, wo)
    if router is not None:
        in_specs += [_const_spec((1, D_MODEL)), _const_spec((LANES, D_MODEL))]
        out_specs += [tok, slab, slab, pl.BlockSpec((None, N_EXPERTS, LANES), lambda i: (i, 0, 0))]
        out_shape += [
            jax.ShapeDtypeStruct((N_TOKENS, D_MODEL), BF16),
            jax.ShapeDtypeStruct((N_TOKENS, LANES), jnp.int32),
            jax.ShapeDtypeStruct((N_TOKENS, LANES), F32),
            jax.ShapeDtypeStruct((N_BLOCKS, N_EXPERTS, LANES), jnp.int32),
        ]
        args += tuple(router)
    return pl.pallas_call(
        functools.partial(_xattn_kernel, route=router is not None),
        grid=(N_BLOCKS,),
        in_specs=in_specs,
        out_specs=out_specs,
        out_shape=out_shape,
        scratch_shapes=[pltpu.VMEM((T, D_MODEL), BF16)],
        compiler_params=_params(),
        name="xattn",
    )(*args)


def _swiglu_rows(h, wgu_ref, wd_ref, d_ff, n_chunks):
    fc = d_ff // n_chunks
    acc = None
    for c in range(n_chunks):
        g = _dot(h, wgu_ref[:, c * fc:(c + 1) * fc])
        up = _dot(h, wgu_ref[:, d_ff + c * fc:d_ff + (c + 1) * fc])
        a = (g * jax.nn.sigmoid(g) * up).astype(BF16)
        part = _dot(a, wd_ref[c * fc:(c + 1) * fc, :])
        acc = part if acc is None else acc + part
    return acc


def _ffn_kernel(x_ref, g_ref, wgu_ref, wd_ref, *rest):
    n_cast = (len(rest) - 1) // 2
    cast_in, o_ref, cast_out = rest[:n_cast], rest[n_cast], rest[n_cast + 1:]
    x = x_ref[...]
    h = _rms(x, g_ref[...]).astype(BF16)
    o_ref[...] = x + _swiglu_rows(h, wgu_ref, wd_ref, D_FF, FF_CHUNKS)
    for src, dst in zip(cast_in, cast_out):
        dst[...] = src[...].astype(BF16)


def _ffn(x, g, wgu, wd, cast_weights=()):
    T = TOK_TILE
    steps = N_TOKENS // T
    tok = pl.BlockSpec((T, D_MODEL), lambda i: (i, 0))
    cast_specs = [pl.BlockSpec((w.shape[0] // steps, w.shape[1]), lambda i: (i, 0))
                  for w in cast_weights]
    return pl.pallas_call(
        _ffn_kernel,
        grid=(steps,),
        in_specs=[tok, _const_spec((1, D_MODEL)), _const_spec((D_MODEL, 2 * D_FF)),
                  _const_spec((D_FF, D_MODEL))] + cast_specs,
        out_specs=[tok] + cast_specs,
        out_shape=[jax.ShapeDtypeStruct((N_TOKENS, D_MODEL), F32)]
                  + [jax.ShapeDtypeStruct(w.shape, BF16) for w in cast_weights],
        compiler_params=_params(),
        name="ffn",
    )(x, g, wgu, wd, *cast_weights)


def _route_rows(x, g_ref, wr_ref, h_ref, meta_ref, gate_ref, cnt_ref):
    T = TOK_TILE
    E = N_EXPERTS
    assert E == SUBLANES
    h = _rms(x, g_ref[...]).astype(BF16)
    h_ref[...] = h
    logits = lax.dot_general(wr_ref[...], h, (((1,), (1,)), ((), ())),
                             preferred_element_type=F32)[0:E]
    ex = lax.broadcasted_iota(jnp.int32, (E, T), 0)
    m1 = jnp.max(logits, axis=0, keepdims=True)
    i1 = jnp.min(jnp.where(logits == m1, ex, E), axis=0, keepdims=True)
    rest = jnp.where(ex == i1, -jnp.inf, logits)
    m2 = jnp.max(rest, axis=0, keepdims=True)
    i2 = jnp.min(jnp.where(rest == m2, ex, E), axis=0, keepdims=True)
    e2 = jnp.exp(m2 - m1)
    g1 = 1.0 / (1.0 + e2)
    g2 = e2 / (1.0 + e2)

    sel1 = ex == i1
    sel2 = ex == i2
    onehot = jnp.where(sel1, 1.0, jnp.where(sel2, 1.0, 0.0))
    earlier = (lax.broadcasted_iota(jnp.int32, (T, T), 0)
               < lax.broadcasted_iota(jnp.int32, (T, T), 1))
    onehot_mxu = jnp.concatenate([onehot, jnp.zeros((E, T), F32)], axis=0).astype(BF16)
    before = _dot(onehot_mxu, jnp.where(earlier, 1.0, 0.0).astype(BF16))[0:E]
    r1 = jnp.sum(jnp.where(sel1, before, 0.0), axis=0, keepdims=True)
    r2 = jnp.sum(jnp.where(sel2, before, 0.0), axis=0, keepdims=True)
    cnt_ref[...] = jnp.broadcast_to(jnp.sum(onehot, axis=1, keepdims=True), (E, LANES)).astype(jnp.int32)

    rows = jnp.where(ex == 0, i1.astype(F32), jnp.where(ex == 1, i2.astype(F32), jnp.where(
        ex == 2, r1, jnp.where(ex == 3, r2, jnp.where(ex == 4, g1, jnp.where(ex == 5, g2, 0.0))))))
    cols = jnp.concatenate([rows, jnp.zeros((LANES - E, T), F32)], axis=0).T
    lane = lax.broadcasted_iota(jnp.int32, (T, LANES), 1)
    meta_ref[...] = cols.astype(jnp.int32)
    gate_ref[...] = jnp.where(lane == 0, cols[:, 4:5], cols[:, 5:6])


def _segment_copies(src, src_row, dst, dst_row, n_rows, max_rows, sem, action):
    size = max_rows
    while size >= SUBLANES:
        done = pl.multiple_of((n_rows // (2 * size)) * (2 * size), SUBLANES)

        @pl.when((n_rows & size) != 0)
        def _(size=size, done=done):
            src_at = 0 if src_row is None else src_row + done
            copy = pltpu.make_async_copy(src.at[pl.ds(src_at, size)],
                                         dst.at[pl.ds(dst_row + done, size)], sem)
            copy.start() if action == "start" else copy.wait()

        size //= 2


def _block_positions(meta_ref, lstart_ref):
    lane = lax.broadcasted_iota(jnp.int32, (TOK_TILE, LANES), 1)
    meta = meta_ref[...]
    lstart = lstart_ref[...]
    pos = []
    for k in range(TOP_K):
        start = jnp.sum(jnp.where(lane == meta[:, k:k + 1], lstart, 0), axis=-1, keepdims=True)
        pos.append(start + meta[:, TOP_K + k:TOP_K + k + 1])
    return pos


def _dispatch_kernel(lstart_s, goff_s, pcnt_s, tail_off_s, tail_len_s, meta_ref, lstart_ref, h_ref,
                     xs_ref, blk_ref, zero_ref, sem, fill_sem):
    b = pl.program_id(0)

    @pl.when(b == 0)
    def _():
        zero_ref[...] = jnp.zeros(zero_ref.shape, F32)
        fill_rows = zero_ref.shape[0]

        def unused_tiles(action):
            def body(i, carry):
                copy = pltpu.make_async_copy(
                    zero_ref, xs_ref.at[pl.ds(pl.multiple_of(i * fill_rows, SUBLANES), fill_rows)],
                    fill_sem)
                copy.start() if action == "start" else copy.wait()
                return carry
            lax.fori_loop(tail_off_s[N_EXPERTS] * (ROW_TILE // fill_rows),
                          N_ROW_TILES * (ROW_TILE // fill_rows), body, 0)

        for action in ("start", "wait"):
            for e in range(N_EXPERTS):
                _segment_copies(zero_ref, None, xs_ref, pl.multiple_of(tail_off_s[e], SUBLANES),
                                tail_len_s[e], fill_rows, fill_sem, action)
            unused_tiles(action)

    def segments(blk, slot, action):
        for e in range(N_EXPERTS):
            _segment_copies(blk_ref.at[slot], pl.multiple_of(lstart_s[blk * N_EXPERTS + e], SUBLANES),
                            xs_ref, pl.multiple_of(goff_s[blk * N_EXPERTS + e], SUBLANES),
                            pcnt_s[blk * N_EXPERTS + e], TOK_TILE, sem.at[slot], action)

    slot = b % 2
    p0, p1 = _block_positions(meta_ref, lstart_ref)
    r = lax.broadcasted_iota(jnp.int32, (TOK_TILE, BLOCK_ROWS), 1)
    sel_t = jnp.where(r == p0, 1.0, jnp.where(r == p1, 1.0, 0.0)).astype(BF16)
    blk_ref[slot] = lax.dot_general(sel_t, h_ref[...], (((0,), (0,)), ((), ())),
                                    preferred_element_type=F32)
    segments(b, slot, "start")

    @pl.when(b > 0)
    def _():
        segments(b - 1, 1 - slot, "wait")

    @pl.when(b == N_BLOCKS - 1)
    def _():
        segments(b, slot, "wait")


def _dispatch(lstart, goff, pcnt, tail_off, tail_len, meta, lstart_rows, h):
    T = TOK_TILE
    grid_spec = pltpu.PrefetchScalarGridSpec(
        num_scalar_prefetch=5,
        grid=(N_BLOCKS,),
        in_specs=[
            pl.BlockSpec((T, LANES), lambda i, *_: (i, 0)),
            pl.BlockSpec((None, 1, LANES), lambda i, *_: (i, 0, 0)),
            pl.BlockSpec((T, D_MODEL), lambda i, *_: (i, 0)),
        ],
        out_specs=pl.BlockSpec(memory_space=pl.ANY),
        scratch_shapes=[pltpu.VMEM((2, BLOCK_ROWS, D_MODEL), F32),
                        pltpu.VMEM((ROW_TILE // 2, D_MODEL), F32),
                        pltpu.SemaphoreType.DMA((2,)),
                        pltpu.SemaphoreType.DMA(())],
    )
    return pl.pallas_call(
        _dispatch_kernel,
        grid_spec=grid_spec,
        out_shape=jax.ShapeDtypeStruct((N_SORTED_ROWS, D_MODEL), F32),
        compiler_params=_params(),
        name="dispatch",
    )(lstart, goff, pcnt, tail_off, tail_len, meta, lstart_rows, h)


def _experts_kernel(tile_e_ref, tile_src_ref, xs_ref, wgu_ref, wd_ref, y_ref):
    del tile_e_ref
    on = tile_src_ref[pl.program_id(0)] == pl.program_id(0)

    @pl.when(on)
    def _():
        y_ref[...] = _swiglu_rows(xs_ref[...].astype(BF16), wgu_ref, wd_ref, D_FF_EXPERT,
                                  EXPERT_FF_CHUNKS)

    @pl.when(jnp.logical_not(on))
    def _():
        y_ref[...] = jnp.zeros((ROW_TILE, D_MODEL), F32)


def _experts(tile_e, tile_src, xs, wgu, wd):
    grid_spec = pltpu.PrefetchScalarGridSpec(
        num_scalar_prefetch=2,
        grid=(N_ROW_TILES,),
        in_specs=[
            pl.BlockSpec((ROW_TILE, D_MODEL), lambda i, te, src: (src[i], 0)),
            pl.BlockSpec((None, D_MODEL, 2 * D_FF_EXPERT), lambda i, te, src: (te[i], 0, 0),
                         pipeline_mode=pl.Buffered(1)),
            pl.BlockSpec((None, D_FF_EXPERT, D_MODEL), lambda i, te, src: (te[i], 0, 0),
                         pipeline_mode=pl.Buffered(1)),
        ],
        out_specs=pl.BlockSpec((ROW_TILE, D_MODEL), lambda i, te, src: (i, 0)),
    )
    return pl.pallas_call(
        _experts_kernel,
        grid_spec=grid_spec,
        out_shape=jax.ShapeDtypeStruct((N_SORTED_ROWS, D_MODEL), F32),
        compiler_params=_params(),
        name="experts",
    )(tile_e, tile_src, xs, wgu, wd)


def _combine_kernel(lstart_s, goff_s, pcnt_s, meta_ref, lstart_ref, x_ref, gate_ref, nf_ref, y_ref,
                    o_ref, buf_ref, sem):
    b = pl.program_id(0)
    slot = b % 2

    def fetch(blk, slot, action):
        for e in range(N_EXPERTS):
            _segment_copies(y_ref, pl.multiple_of(goff_s[blk * N_EXPERTS + e], SUBLANES),
                            buf_ref.at[slot], pl.multiple_of(lstart_s[blk * N_EXPERTS + e], SUBLANES),
                            pcnt_s[blk * N_EXPERTS + e], TOK_TILE, sem.at[slot], action)

    def prefetch(blk, slot):
        buf_ref[slot] = jnp.zeros(buf_ref.shape[1:], F32)
        fetch(blk, slot, "start")

    @pl.when(b == 0)
    def _():
        prefetch(0, 0)

    @pl.when(b + 1 < N_BLOCKS)
    def _():
        prefetch(b + 1, 1 - slot)

    fetch(b, slot, "wait")

    p0, p1 = _block_positions(meta_ref, lstart_ref)
    r = lax.broadcasted_iota(jnp.int32, (TOK_TILE, BLOCK_ROWS), 1)
    gates = gate_ref[...]
    y = buf_ref[slot].astype(BF16)
    pick0 = jnp.where(r == p0, 1.0, 0.0).astype(BF16)
    pick1 = jnp.where(r == p1, 1.0, 0.0).astype(BF16)
    moe = gates[:, 0:1] * _dot(pick0, y) + gates[:, 1:2] * _dot(pick1, y)
    o_ref[...] = _rms(x_ref[...] + moe, nf_ref[...])


def _combine(lstart, goff, pcnt, meta, lstart_rows, x, gates, norm_final, y):
    T = TOK_TILE
    grid_spec = pltpu.PrefetchScalarGridSpec(
        num_scalar_prefetch=3,
        grid=(N_BLOCKS,),
        in_specs=[
            pl.BlockSpec((T, LANES), lambda i, *_: (i, 0)),
            pl.BlockSpec((None, 1, LANES), lambda i, *_: (i, 0, 0)),
            pl.BlockSpec((T, D_MODEL), lambda i, *_: (i, 0)),
            pl.BlockSpec((T, LANES), lambda i, *_: (i, 0)),
            pl.BlockSpec((1, D_MODEL), lambda i, *_: (0, 0)),
            pl.BlockSpec(memory_space=pl.ANY),
        ],
        out_specs=pl.BlockSpec((T, D_MODEL), lambda i, *_: (i, 0)),
        scratch_shapes=[pltpu.VMEM((2, BLOCK_ROWS, D_MODEL), F32), pltpu.SemaphoreType.DMA((2,))],
    )
    return pl.pallas_call(
        _combine_kernel,
        grid_spec=grid_spec,
        out_shape=jax.ShapeDtypeStruct((N_TOKENS, D_MODEL), F32),
        compiler_params=_params(),
        name="combine",
    )(lstart, goff, pcnt, meta, lstart_rows, x, gates, norm_final, y)


def _moe_layer(x, h, meta, gates, counts, w_gu, w_down, norm_final):
    i32 = lambda a: a.astype(jnp.int32)
    pcnt = -(-counts[:, :, 0] // SUBLANES) * SUBLANES
    lstart = jnp.cumsum(pcnt, axis=1) - pcnt
    rows_e = jnp.sum(pcnt, axis=0)
    tiles_e = -(-rows_e // ROW_TILE)
    tile_end = jnp.cumsum(tiles_e)
    region = (tile_end - tiles_e) * ROW_TILE
    goff = region[None, :] + jnp.cumsum(pcnt, axis=0) - pcnt
    tail_off = jnp.concatenate([region + rows_e, tile_end[-1:]])
    tail_len = tiles_e * ROW_TILE - rows_e
    tile_ids = jnp.arange(N_ROW_TILES, dtype=jnp.int32)
    tile_src = jnp.minimum(tile_ids, jnp.maximum(tile_end[-1] - 1, 0))
    tile_e = jnp.minimum(jnp.sum(tile_src[:, None] >= tile_end[None, :], axis=1), N_EXPERTS - 1)
    lstart_rows = jnp.pad(lstart, ((0, 0), (0, LANES - N_EXPERTS)))[:, None, :]
    flat = lambda a: i32(a).reshape(-1)

    xs = _dispatch(flat(lstart), flat(goff), flat(pcnt), i32(tail_off), i32(tail_len), meta,
                   i32(lstart_rows), h)
    y = _experts(i32(tile_e), i32(tile_src), xs, w_gu, w_down)
    return _combine(flat(lstart), flat(goff), flat(pcnt), meta, i32(lstart_rows), x, gates,
                    norm_final, y)


def kernel(x, mem, norm_mem, norm_mix, w_in, pool_mix, pool_scale, sgu_ln_g, sgu_ln_b, w_spatial, b_spatial, w_branch_a, w_branch_b, w_out, norm_xattn, w_xq, w_xkv, w_xo, norm_ffn, w_ff_gu, w_ff_down, w_router, w_moe_gu, w_moe_down, norm_final):
    assert DEPTH == 2 and x.shape == (BATCH, SEQ, D_MODEL)
    row = lambda a: a.reshape(1, -1)
    bf = lambda a: a.astype(BF16)

    xt = x.reshape(N_TOKENS, D_MODEL)
    kv = _kv(mem.reshape(BATCH * MEM_LEN, D_MODEL), row(norm_mem), bf(w_xkv))
    for l in range(DEPTH):
        bsp_tile = jnp.repeat(b_spatial[l].T, SGU_GROUP_DIM, axis=1)
        xt = _mixer(xt, row(norm_mix[l]), bf(w_in[l]), bf(pool_mix[l]), row(pool_scale[l]),
                    row(sgu_ln_g[l]), row(sgu_ln_b[l]), w_spatial[l], bsp_tile,
                    bf(w_branch_a[l]), bf(w_branch_b[l]), bf(w_out[l]))
        attn = (xt, row(norm_xattn[l]), bf(w_xq[l]), kv[l], bf(w_xo[l]))
        if l % 2 == 0:
            (xt,) = _xattn(*attn)
            m = (l + 1) // 2
            xt, moe_gu, moe_down = _ffn(
                xt, row(norm_ffn[l]), bf(w_ff_gu[l // 2]), bf(w_ff_down[l // 2]),
                cast_weights=(w_moe_gu[m].reshape(N_EXPERTS * D_MODEL, 2 * D_FF_EXPERT),
                              w_moe_down[m].reshape(N_EXPERTS * D_FF_EXPERT, D_MODEL)))
        else:
            wr_pad = bf(jnp.pad(w_router[l // 2].T, ((0, LANES - N_EXPERTS), (0, 0))))
            xt, h, meta, gates, counts = _xattn(*attn, router=(row(norm_ffn[l]), wr_pad))
            xt = _moe_layer(xt, h, meta, gates, counts,
                            moe_gu.reshape(N_EXPERTS, D_MODEL, 2 * D_FF_EXPERT),
                            moe_down.reshape(N_EXPERTS, D_FF_EXPERT, D_MODEL), row(norm_final))
    return xt.reshape(BATCH, SEQ, D_MODEL)
```

```python
import functools

import jax
import jax.numpy as jnp
from jax import lax
from jax.experimental import pallas as pl
from jax.experimental.pallas import tpu as pltpu

D_MODEL = 1024
BATCH = 8
SEQ = 4096
DEPTH = 2
MEM_LEN = 256
POOL_WIDTH = 512
POOL_GROUPS = 4
POOL_WINDOWS = (2, 4, 8, 16)
POOL_GROUP_DIM = POOL_WIDTH // POOL_GROUPS
SGU_WIDTH = 512
SGU_GROUPS = 4
SGU_GROUP_DIM = SGU_WIDTH // SGU_GROUPS
CHUNK = 128
IN_COLS = POOL_WIDTH + 2 * SGU_WIDTH + 2 * D_MODEL
XATTN_HEADS = 4
XATTN_HEAD_DIM = D_MODEL // XATTN_HEADS
D_FF = 2816
N_EXPERTS = 8
TOP_K = 2
D_FF_EXPERT = 3584
EPS = 1e-6

N_TOKENS = BATCH * SEQ
LANES = 128
POOL_HALO = 16
TOK_TILE = 512
MIXER_SLAB = 512
ROW_TILE = 512
SUBLANES = 8
N_BLOCKS = N_TOKENS // TOK_TILE
BLOCK_ROWS = -(-(TOK_TILE * TOP_K + N_EXPERTS * (SUBLANES - 1)) // LANES) * LANES
MAX_SORTED = N_TOKENS * TOP_K + N_BLOCKS * N_EXPERTS * (SUBLANES - 1)
N_ROW_TILES = -(-MAX_SORTED // ROW_TILE) + N_EXPERTS
N_SORTED_ROWS = N_ROW_TILES * ROW_TILE
FF_CHUNKS = 2
EXPERT_FF_CHUNKS = 2
VMEM_LIMIT = 52 * 1024 * 1024

BF16 = jnp.bfloat16
F32 = jnp.float32


def _dot(a, b):
    return jnp.dot(a, b, preferred_element_type=F32)


def _rms(x, g):
    return x * lax.rsqrt(jnp.mean(x * x, axis=-1, keepdims=True) + EPS) * g


def _const_spec(shape):
    return pl.BlockSpec(shape, lambda *_: (0,) * len(shape), pipeline_mode=pl.Buffered(1))


def _params(n_axes=1):
    return pltpu.CompilerParams(dimension_semantics=("arbitrary",) * n_axes,
                                vmem_limit_bytes=VMEM_LIMIT)


def _kv_kernel(mem_ref, g_ref, w_ref, o_ref):
    mem_n = _rms(mem_ref[...], g_ref[...]).astype(BF16)
    o_ref[...] = _dot(mem_n, w_ref[...]).astype(BF16)


def _kv(mem2d, norm_mem, w_xkv):
    return pl.pallas_call(
        _kv_kernel,
        grid=(DEPTH, BATCH),
        in_specs=[
            pl.BlockSpec((MEM_LEN, D_MODEL), lambda l, b: (b, 0)),
            pl.BlockSpec((1, D_MODEL), lambda l, b: (0, 0)),
            pl.BlockSpec((None, D_MODEL, 2 * D_MODEL), lambda l, b: (l, 0, 0)),
        ],
        out_specs=pl.BlockSpec((None, MEM_LEN, 2 * D_MODEL), lambda l, b: (l, b, 0)),
        out_shape=jax.ShapeDtypeStruct((DEPTH, BATCH * MEM_LEN, 2 * D_MODEL), BF16),
        compiler_params=_params(2),
        name="kv",
    )(mem2d, norm_mem, w_xkv)


def _mixer_kernel(x_ref, nm_ref, win_ref, pmix_ref, pscale_ref, lng_ref, lnb_ref, wsp_ref,
                  bsp_ref, wa_ref, wb_ref, wout_ref, o_ref, pext_ref, ain_ref, vn_ref, sg_ref):
    T = TOK_TILE
    R = MIXER_SLAB
    G = POOL_GROUP_DIM
    tile_in_seq = pl.program_id(0) % (SEQ // T)

    @pl.when(tile_in_seq == 0)
    def _():
        pext_ref[0:POOL_HALO, :] = jnp.zeros((POOL_HALO, POOL_WIDTH), F32)

    causal = (lax.broadcasted_iota(jnp.int32, (CHUNK, CHUNK), 0)
              >= lax.broadcasted_iota(jnp.int32, (CHUNK, CHUNK), 1))
    w_sp = [jnp.where(causal, wsp_ref[gi], 0.0).astype(BF16) for gi in range(SGU_GROUPS)]

    for r0 in range(0, T, R):
        rows = slice(r0, r0 + R)
        x = x_ref[rows, :]
        h = _rms(x, nm_ref[...]).astype(BF16)

        c0 = POOL_WIDTH + 2 * SGU_WIDTH
        p = _dot(h, win_ref[:, 0:POOL_WIDTH])
        u = _dot(h, win_ref[:, POOL_WIDTH:POOL_WIDTH + SGU_WIDTH])
        v = _dot(h, win_ref[:, POOL_WIDTH + SGU_WIDTH:c0])
        ga = _dot(h, win_ref[:, c0:c0 + D_MODEL])
        gb = _dot(h, win_ref[:, c0 + D_MODEL:c0 + 2 * D_MODEL])
        gate_a = jax.nn.sigmoid(ga)
        gate_b = jax.nn.sigmoid(gb)

        pext_ref[POOL_HALO + r0:POOL_HALO + r0 + R, :] = p
        t_pos = tile_in_seq * T + r0 + lax.broadcasted_iota(jnp.int32, (R, G), 0)
        for gi, w in enumerate(POOL_WINDOWS):
            cols = slice(gi * G, (gi + 1) * G)
            pg = p[:, cols]
            acc = pg
            for j in range(1, w):
                acc = acc + pext_ref[POOL_HALO + r0 - j:POOL_HALO + r0 - j + R, cols]
            cnt = jnp.minimum(t_pos + 1, w).astype(F32)
            pooled = acc / cnt - pg
            mixed = _dot(pooled.astype(BF16), pmix_ref[gi])
            ain_ref[rows, cols] = (mixed * pscale_ref[:, cols]).astype(BF16)
        br_a = _dot(ain_ref[rows, :], wa_ref[...])

        u = jax.nn.gelu(u)
        v = jax.nn.gelu(v)
        mu = jnp.mean(v, axis=-1, keepdims=True)
        vc = v - mu
        vn = (vc * lax.rsqrt(jnp.mean(vc * vc, axis=-1, keepdims=True) + EPS) * lng_ref[...]
              + lnb_ref[...])
        vn_ref[rows, :] = vn.astype(BF16)
        n_chunks = R // CHUNK
        for gi in range(SGU_GROUPS):
            cols = slice(gi * SGU_GROUP_DIM, (gi + 1) * SGU_GROUP_DIM)
            rhs = jnp.concatenate([vn_ref[r0 + c * CHUNK:r0 + (c + 1) * CHUNK, cols]
                                   for c in range(n_chunks)], axis=1)
            mixed = _dot(w_sp[gi], rhs)
            for c in range(n_chunks):
                crow = slice(c * CHUNK, (c + 1) * CHUNK)
                blk = mixed[:, c * SGU_GROUP_DIM:(c + 1) * SGU_GROUP_DIM] + bsp_ref[:, cols]
                sg_ref[r0 + c * CHUNK:r0 + (c + 1) * CHUNK, cols] = (u[crow, cols] * blk).astype(BF16)
        br_b = _dot(sg_ref[rows, :], wb_ref[...])

        merged = gate_a * br_a + gate_b * br_b
        o_ref[rows, :] = x + _dot(merged.astype(BF16), wout_ref[...])

    pext_ref[0:POOL_HALO, :] = pext_ref[T:T + POOL_HALO, :]


def _mixer(x, nm, win, pmix, pscale, lng, lnb, wsp, bsp_tile, wa, wb, wout):
    T = TOK_TILE
    tok = pl.BlockSpec((T, D_MODEL), lambda i: (i, 0))
    return pl.pallas_call(
        _mixer_kernel,
        grid=(N_TOKENS // T,),
        in_specs=[
            tok,
            _const_spec((1, D_MODEL)),
            _const_spec((D_MODEL, IN_COLS)),
            _const_spec((POOL_GROUPS, POOL_GROUP_DIM, POOL_GROUP_DIM)),
            _const_spec((1, POOL_WIDTH)),
            _const_spec((1, SGU_WIDTH)),
            _const_spec((1, SGU_WIDTH)),
            _const_spec((SGU_GROUPS, CHUNK, CHUNK)),
            _const_spec((CHUNK, SGU_WIDTH)),
            _const_spec((POOL_WIDTH, D_MODEL)),
            _const_spec((SGU_WIDTH, D_MODEL)),
            _const_spec((D_MODEL, D_MODEL)),
        ],
        out_specs=tok,
        out_shape=jax.ShapeDtypeStruct((N_TOKENS, D_MODEL), F32),
        scratch_shapes=[
            pltpu.VMEM((POOL_HALO + T, POOL_WIDTH), F32),
            pltpu.VMEM((T, POOL_WIDTH), BF16),
            pltpu.VMEM((T, SGU_WIDTH), BF16),
            pltpu.VMEM((T, SGU_WIDTH), BF16),
        ],
        compiler_params=_params(),
        name="mixer",
    )(x, nm, win, pmix, pscale, lng, lnb, wsp, bsp_tile, wa, wb, wout)


def _xattn_kernel(x_ref, g_ref, wq_ref, kv_ref, wo_ref, *rest, route):
    if route:
        gf_ref, wr_ref, o_ref, h_ref, meta_ref, rows_ref, cnt_ref, ocat_ref = rest
    else:
        o_ref, ocat_ref = rest
    x = x_ref[...]
    h = _rms(x, g_ref[...]).astype(BF16)
    q = (_dot(h, wq_ref[...]) * (XATTN_HEAD_DIM ** -0.5)).astype(BF16)
    head_cols = [slice(hd * XATTN_HEAD_DIM, (hd + 1) * XATTN_HEAD_DIM) for hd in range(XATTN_HEADS)]
    scores = [lax.dot_general(q[:, cols], kv_ref[:, cols], (((1,), (1,)), ((), ())),
                              preferred_element_type=F32) for cols in head_cols]
    for hd, cols in enumerate(head_cols):
        v = kv_ref[:, D_MODEL + hd * XATTN_HEAD_DIM:D_MODEL + (hd + 1) * XATTN_HEAD_DIM]
        s = scores[hd]
        e = jnp.exp(s - jnp.max(s, axis=-1, keepdims=True))
        pr = e / jnp.sum(e, axis=-1, keepdims=True)
        ocat_ref[:, cols] = _dot(pr.astype(BF16), v).astype(BF16)
    x = x + _dot(ocat_ref[...], wo_ref[...])
    o_ref[...] = x
    if route:
        _route_rows(x, gf_ref, wr_ref, h_ref, meta_ref, rows_ref, cnt_ref)


def _xattn(x, g, wq, kv_l, wo, router=None):
    T = TOK_TILE
    tok = pl.BlockSpec((T, D_MODEL), lambda i: (i, 0))
    slab = pl.BlockSpec((T, LANES), lambda i: (i, 0))
    in_specs = [
        tok,
        _const_spec((1, D_MODEL)),
        _const_spec((D_MODEL, D_MODEL)),
        pl.BlockSpec((MEM_LEN, 2 * D_MODEL), lambda i: (i // (SEQ // T), 0)),
        _const_spec((D_MODEL, D_MODEL)),
    ]
    out_specs = [tok]
    out_shape = [jax.ShapeDtypeStruct((N_TOKENS, D_MODEL), F32)]
    args = (x, g, wq, kv_l, wo)
    if router is not None:
        in_specs += [_const_spec((1, D_MODEL)), _const_spec((LANES, D_MODEL))]
        out_specs += [tok, slab, pl.BlockSpec((None, N_EXPERTS, T), lambda i: (i, 0, 0)),
                      pl.BlockSpec((None, N_EXPERTS, LANES), lambda i: (i, 0, 0))]
        out_shape += [
            jax.ShapeDtypeStruct((N_TOKENS, D_MODEL), BF16),
            jax.ShapeDtypeStruct((N_TOKENS, LANES), jnp.int32),
            jax.ShapeDtypeStruct((N_BLOCKS, N_EXPERTS, T), F32),
            jax.ShapeDtypeStruct((N_BLOCKS, N_EXPERTS, LANES), jnp.int32),
        ]
        args += tuple(router)
    return pl.pallas_call(
        functools.partial(_xattn_kernel, route=router is not None),
        grid=(N_BLOCKS,),
        in_specs=in_specs,
        out_specs=out_specs,
        out_shape=out_shape,
        scratch_shapes=[pltpu.VMEM((T, D_MODEL), BF16)],
        compiler_params=_params(),
        name="xattn",
    )(*args)


def _swiglu_rows(h, wgu_ref, wd_ref, d_ff, n_chunks):
    fc = d_ff // n_chunks
    acc = None
    for c in range(n_chunks):
        g = _dot(h, wgu_ref[:, c * fc:(c + 1) * fc])
        up = _dot(h, wgu_ref[:, d_ff + c * fc:d_ff + (c + 1) * fc])
        a = (g * jax.nn.sigmoid(g) * up).astype(BF16)
        part = _dot(a, wd_ref[c * fc:(c + 1) * fc, :])
        acc = part if acc is None else acc + part
    return acc


def _ffn_kernel(x_ref, g_ref, wgu_ref, wd_ref, *rest):
    n_cast = (len(rest) - 1) // 2
    cast_in, o_ref, cast_out = rest[:n_cast], rest[n_cast], rest[n_cast + 1:]
    x = x_ref[...]
    h = _rms(x, g_ref[...]).astype(BF16)
    o_ref[...] = x + _swiglu_rows(h, wgu_ref, wd_ref, D_FF, FF_CHUNKS)
    for src, dst in zip(cast_in, cast_out):
        dst[...] = src[...].astype(BF16)


def _ffn(x, g, wgu, wd, cast_weights=()):
    T = TOK_TILE
    steps = N_TOKENS // T
    tok = pl.BlockSpec((T, D_MODEL), lambda i: (i, 0))
    cast_specs = [pl.BlockSpec((w.shape[0] // steps, w.shape[1]), lambda i: (i, 0))
                  for w in cast_weights]
    return pl.pallas_call(
        _ffn_kernel,
        grid=(steps,),
        in_specs=[tok, _const_spec((1, D_MODEL)), _const_spec((D_MODEL, 2 * D_FF)),
                  _const_spec((D_FF, D_MODEL))] + cast_specs,
        out_specs=[tok] + cast_specs,
        out_shape=[jax.ShapeDtypeStruct((N_TOKENS, D_MODEL), F32)]
                  + [jax.ShapeDtypeStruct(w.shape, BF16) for w in cast_weights],
        compiler_params=_params(),
        name="ffn",
    )(x, g, wgu, wd, *cast_weights)


def _route_rows(x, g_ref, wr_ref, h_ref, meta_ref, rows_ref, cnt_ref):
    T = TOK_TILE
    E = N_EXPERTS
    assert E == SUBLANES
    h = _rms(x, g_ref[...]).astype(BF16)
    h_ref[...] = h
    logits = lax.dot_general(wr_ref[...], h, (((1,), (1,)), ((), ())),
                             preferred_element_type=F32)[0:E]
    ex = lax.broadcasted_iota(jnp.int32, (E, T), 0)
    m1 = jnp.max(logits, axis=0, keepdims=True)
    i1 = jnp.min(jnp.where(logits == m1, ex, E), axis=0, keepdims=True)
    rest = jnp.where(ex == i1, -jnp.inf, logits)
    m2 = jnp.max(rest, axis=0, keepdims=True)
    i2 = jnp.min(jnp.where(rest == m2, ex, E), axis=0, keepdims=True)
    e2 = jnp.exp(m2 - m1)
    g1 = 1.0 / (1.0 + e2)
    g2 = e2 / (1.0 + e2)

    sel1 = ex == i1
    sel2 = ex == i2
    onehot = jnp.where(sel1, 1.0, jnp.where(sel2, 1.0, 0.0))
    earlier = (lax.broadcasted_iota(jnp.int32, (T, T), 0)
               < lax.broadcasted_iota(jnp.int32, (T, T), 1))
    onehot_mxu = jnp.concatenate([onehot, jnp.zeros((E, T), F32)], axis=0).astype(BF16)
    before = _dot(onehot_mxu, jnp.where(earlier, 1.0, 0.0).astype(BF16))[0:E]
    r1 = jnp.sum(jnp.where(sel1, before, 0.0), axis=0, keepdims=True)
    r2 = jnp.sum(jnp.where(sel2, before, 0.0), axis=0, keepdims=True)
    cnt_ref[...] = jnp.broadcast_to(jnp.sum(onehot, axis=1, keepdims=True), (E, LANES)).astype(jnp.int32)

    rows = jnp.where(ex == 0, i1.astype(F32), jnp.where(ex == 1, i2.astype(F32), jnp.where(
        ex == 2, r1, jnp.where(ex == 3, r2, jnp.where(ex == 4, g1, jnp.where(ex == 5, g2, 0.0))))))
    cols = jnp.concatenate([rows, jnp.zeros((LANES - E, T), F32)], axis=0).T
    meta_ref[...] = cols.astype(jnp.int32)
    rows_ref[...] = rows


def _segment_copies(src, src_row, dst, dst_row, n_rows, max_rows, sem, action):
    size = max_rows
    while size >= SUBLANES:
        done = pl.multiple_of((n_rows // (2 * size)) * (2 * size), SUBLANES)

        @pl.when((n_rows & size) != 0)
        def _(size=size, done=done):
            src_at = 0 if src_row is None else src_row + done
            copy = pltpu.make_async_copy(src.at[pl.ds(src_at, size)],
                                         dst.at[pl.ds(dst_row + done, size)], sem)
            copy.start() if action == "start" else copy.wait()

        size //= 2


def _block_positions_rows(rows_ref, lstart_s, b):
    pos = []
    for k in range(TOP_K):
        expert = rows_ref[k:k + 1, :].astype(jnp.int32)
        start = jnp.zeros_like(expert)
        for e in range(N_EXPERTS):
            start = jnp.where(expert == e, lstart_s[b * N_EXPERTS + e], start)
        pos.append(start + rows_ref[TOP_K + k:TOP_K + k + 1, :].astype(jnp.int32))
    return pos


def _block_positions(meta_ref, lstart_ref):
    lane = lax.broadcasted_iota(jnp.int32, (TOK_TILE, LANES), 1)
    meta = meta_ref[...]
    lstart = lstart_ref[...]
    pos = []
    for k in range(TOP_K):
        start = jnp.sum(jnp.where(lane == meta[:, k:k + 1], lstart, 0), axis=-1, keepdims=True)
        pos.append(start + meta[:, TOP_K + k:TOP_K + k + 1])
    return pos


def _dispatch_kernel(lstart_s, goff_s, pcnt_s, tail_off_s, tail_len_s, rows_ref, h_ref,
                     xs_ref, blk_ref, zero_ref, sem, fill_sem):
    b = pl.program_id(0)

    @pl.when(b == 0)
    def _():
        zero_ref[...] = jnp.zeros(zero_ref.shape, F32)
        fill_rows = zero_ref.shape[0]

        def unused_tiles(action):
            def body(i, carry):
                copy = pltpu.make_async_copy(
                    zero_ref, xs_ref.at[pl.ds(pl.multiple_of(i * fill_rows, SUBLANES), fill_rows)],
                    fill_sem)
                copy.start() if action == "start" else copy.wait()
                return carry
            lax.fori_loop(tail_off_s[N_EXPERTS] * (ROW_TILE // fill_rows),
                          N_ROW_TILES * (ROW_TILE // fill_rows), body, 0)

        for action in ("start", "wait"):
            for e in range(N_EXPERTS):
                _segment_copies(zero_ref, None, xs_ref, pl.multiple_of(tail_off_s[e], SUBLANES),
                                tail_len_s[e], fill_rows, fill_sem, action)
            unused_tiles(action)

    def segments(blk, slot, action):
        for e in range(N_EXPERTS):
            _segment_copies(blk_ref.at[slot], pl.multiple_of(lstart_s[blk * N_EXPERTS + e], SUBLANES),
                            xs_ref, pl.multiple_of(goff_s[blk * N_EXPERTS + e], SUBLANES),
                            pcnt_s[blk * N_EXPERTS + e], TOK_TILE, sem.at[slot], action)

    slot = b % 2
    p0, p1 = _block_positions_rows(rows_ref, lstart_s, b)
    r = lax.broadcasted_iota(jnp.int32, (BLOCK_ROWS, TOK_TILE), 0)
    sel = jnp.where(r == p0, 1.0, jnp.where(r == p1, 1.0, 0.0)).astype(BF16)
    blk_ref[slot] = _dot(sel, h_ref[...])
    segments(b, slot, "start")

    @pl.when(b > 0)
    def _():
        segments(b - 1, 1 - slot, "wait")

    @pl.when(b == N_BLOCKS - 1)
    def _():
        segments(b, slot, "wait")


def _dispatch(lstart, goff, pcnt, tail_off, tail_len, rows, h):
    T = TOK_TILE
    grid_spec = pltpu.PrefetchScalarGridSpec(
        num_scalar_prefetch=5,
        grid=(N_BLOCKS,),
        in_specs=[
            pl.BlockSpec((None, N_EXPERTS, T), lambda i, *_: (i, 0, 0)),
            pl.BlockSpec((T, D_MODEL), lambda i, *_: (i, 0)),
        ],
        out_specs=pl.BlockSpec(memory_space=pl.ANY),
        scratch_shapes=[pltpu.VMEM((2, BLOCK_ROWS, D_MODEL), F32),
                        pltpu.VMEM((ROW_TILE // 2, D_MODEL), F32),
                        pltpu.SemaphoreType.DMA((2,)),
                        pltpu.SemaphoreType.DMA(())],
    )
    return pl.pallas_call(
        _dispatch_kernel,
        grid_spec=grid_spec,
        out_shape=jax.ShapeDtypeStruct((N_SORTED_ROWS, D_MODEL), F32),
        compiler_params=_params(),
        name="dispatch",
    )(lstart, goff, pcnt, tail_off, tail_len, rows, h)


def _experts_kernel(tile_e_ref, tile_src_ref, xs_ref, wgu_ref, wd_ref, y_ref):
    del tile_e_ref
    on = tile_src_ref[pl.program_id(0)] == pl.program_id(0)

    @pl.when(on)
    def _():
        y_ref[...] = _swiglu_rows(xs_ref[...].astype(BF16), wgu_ref, wd_ref, D_FF_EXPERT,
                                  EXPERT_FF_CHUNKS)

    @pl.when(jnp.logical_not(on))
    def _():
        y_ref[...] = jnp.zeros((ROW_TILE, D_MODEL), F32)


def _experts(tile_e, tile_src, xs, wgu, wd):
    grid_spec = pltpu.PrefetchScalarGridSpec(
        num_scalar_prefetch=2,
        grid=(N_ROW_TILES,),
        in_specs=[
            pl.BlockSpec((ROW_TILE, D_MODEL), lambda i, te, src: (src[i], 0)),
            pl.BlockSpec((None, D_MODEL, 2 * D_FF_EXPERT), lambda i, te, src: (te[i], 0, 0),
                         pipeline_mode=pl.Buffered(1)),
            pl.BlockSpec((None, D_FF_EXPERT, D_MODEL), lambda i, te, src: (te[i], 0, 0),
                         pipeline_mode=pl.Buffered(1)),
        ],
        out_specs=pl.BlockSpec((ROW_TILE, D_MODEL), lambda i, te, src: (i, 0)),
    )
    return pl.pallas_call(
        _experts_kernel,
        grid_spec=grid_spec,
        out_shape=jax.ShapeDtypeStruct((N_SORTED_ROWS, D_MODEL), F32),
        compiler_params=_params(),
        name="experts",
    )(tile_e, tile_src, xs, wgu, wd)


def _combine_kernel(lstart_s, goff_s, pcnt_s, meta_ref, lstart_ref, rows_ref, x_ref, nf_ref, y_ref,
                    o_ref, buf_ref, sem):
    b = pl.program_id(0)
    slot = b % 2

    def fetch(blk, slot, action):
        for e in range(N_EXPERTS):
            _segment_copies(y_ref, pl.multiple_of(goff_s[blk * N_EXPERTS + e], SUBLANES),
                            buf_ref.at[slot], pl.multiple_of(lstart_s[blk * N_EXPERTS + e], SUBLANES),
                            pcnt_s[blk * N_EXPERTS + e], TOK_TILE, sem.at[slot], action)

    def prefetch(blk, slot):
        buf_ref[slot] = jnp.zeros(buf_ref.shape[1:], F32)
        fetch(blk, slot, "start")

    @pl.when(b == 0)
    def _():
        prefetch(0, 0)

    @pl.when(b + 1 < N_BLOCKS)
    def _():
        prefetch(b + 1, 1 - slot)

    fetch(b, slot, "wait")

    q0, q1 = _block_positions_rows(rows_ref, lstart_s, b)
    rr = lax.broadcasted_iota(jnp.int32, (BLOCK_ROWS, TOK_TILE), 0)
    row_gate = jnp.sum(jnp.where(rr == q0, rows_ref[2 * TOP_K:2 * TOP_K + 1, :],
                                 jnp.where(rr == q1, rows_ref[2 * TOP_K + 1:2 * TOP_K + 2, :], 0.0)),
                       axis=1, keepdims=True)
    y = (buf_ref[slot] * row_gate).astype(BF16)

    p0, p1 = _block_positions(meta_ref, lstart_ref)
    r = lax.broadcasted_iota(jnp.int32, (TOK_TILE, BLOCK_ROWS), 1)
    pick = jnp.where(r == p0, 1.0, jnp.where(r == p1, 1.0, 0.0)).astype(BF16)
    o_ref[...] = _rms(x_ref[...] + _dot(pick, y), nf_ref[...])


def _combine(lstart, goff, pcnt, meta, lstart_rows, rows, x, norm_final, y):
    T = TOK_TILE
    grid_spec = pltpu.PrefetchScalarGridSpec(
        num_scalar_prefetch=3,
        grid=(N_BLOCKS,),
        in_specs=[
            pl.BlockSpec((T, LANES), lambda i, *_: (i, 0)),
            pl.BlockSpec((None, 1, LANES), lambda i, *_: (i, 0, 0)),
            pl.BlockSpec((None, N_EXPERTS, T), lambda i, *_: (i, 0, 0)),
            pl.BlockSpec((T, D_MODEL), lambda i, *_: (i, 0)),
            pl.BlockSpec((1, D_MODEL), lambda i, *_: (0, 0)),
            pl.BlockSpec(memory_space=pl.ANY),
        ],
        out_specs=pl.BlockSpec((T, D_MODEL), lambda i, *_: (i, 0)),
        scratch_shapes=[pltpu.VMEM((2, BLOCK_ROWS, D_MODEL), F32), pltpu.SemaphoreType.DMA((2,))],
    )
    return pl.pallas_call(
        _combine_kernel,
        grid_spec=grid_spec,
        out_shape=jax.ShapeDtypeStruct((N_TOKENS, D_MODEL), F32),
        compiler_params=_params(),
        name="combine",
    )(lstart, goff, pcnt, meta, lstart_rows, rows, x, norm_final, y)


def _moe_layer(x, h, meta, rows, counts, w_gu, w_down, norm_final):
    i32 = lambda a: a.astype(jnp.int32)
    pcnt = -(-counts[:, :, 0] // SUBLANES) * SUBLANES
    lstart = jnp.cumsum(pcnt, axis=1) - pcnt
    rows_e = jnp.sum(pcnt, axis=0)
    tiles_e = -(-rows_e // ROW_TILE)
    tile_end = jnp.cumsum(tiles_e)
    region = (tile_end - tiles_e) * ROW_TILE
    goff = region[None, :] + jnp.cumsum(pcnt, axis=0) - pcnt
    tail_off = jnp.concatenate([region + rows_e, tile_end[-1:]])
    tail_len = tiles_e * ROW_TILE - rows_e
    tile_ids = jnp.arange(N_ROW_TILES, dtype=jnp.int32)
    tile_src = jnp.minimum(tile_ids, jnp.maximum(tile_end[-1] - 1, 0))
    tile_e = jnp.minimum(jnp.sum(tile_src[:, None] >= tile_end[None, :], axis=1), N_EXPERTS - 1)
    lstart_rows = jnp.pad(lstart, ((0, 0), (0, LANES - N_EXPERTS)))[:, None, :]
    flat = lambda a: i32(a).reshape(-1)

    xs = _dispatch(flat(lstart), flat(goff), flat(pcnt), i32(tail_off), i32(tail_len), rows, h)
    y = _experts(i32(tile_e), i32(tile_src), xs, w_gu, w_down)
    return _combine(flat(lstart), flat(goff), flat(pcnt), meta, i32(lstart_rows), rows, x,
                    norm_final, y)


def kernel(x, mem, norm_mem, norm_mix, w_in, pool_mix, pool_scale, sgu_ln_g, sgu_ln_b, w_spatial, b_spatial, w_branch_a, w_branch_b, w_out, norm_xattn, w_xq, w_xkv, w_xo, norm_ffn, w_ff_gu, w_ff_down, w_router, w_moe_gu, w_moe_down, norm_final):
    assert DEPTH == 2 and x.shape == (BATCH, SEQ, D_MODEL)
    row = lambda a: a.reshape(1, -1)
    bf = lambda a: a.astype(BF16)

    xt = x.reshape(N_TOKENS, D_MODEL)
    kv = _kv(mem.reshape(BATCH * MEM_LEN, D_MODEL), row(norm_mem), bf(w_xkv))
    for l in range(DEPTH):
        bsp_tile = jnp.repeat(b_spatial[l].T, SGU_GROUP_DIM, axis=1)
        xt = _mixer(xt, row(norm_mix[l]), bf(w_in[l]), bf(pool_mix[l]), row(pool_scale[l]),
                    row(sgu_ln_g[l]), row(sgu_ln_b[l]), w_spatial[l], bsp_tile,
                    bf(w_branch_a[l]), bf(w_branch_b[l]), bf(w_out[l]))
        attn = (xt, row(norm_xattn[l]), bf(w_xq[l]), kv[l], bf(w_xo[l]))
        if l % 2 == 0:
            (xt,) = _xattn(*attn)
            m = (l + 1) // 2
            xt, moe_gu, moe_down = _ffn(
                xt, row(norm_ffn[l]), bf(w_ff_gu[l // 2]), bf(w_ff_down[l // 2]),
                cast_weights=(w_moe_gu[m].reshape(N_EXPERTS * D_MODEL, 2 * D_FF_EXPERT),
                              w_moe_down[m].reshape(N_EXPERTS * D_FF_EXPERT, D_MODEL)))
        else:
            wr_pad = bf(jnp.pad(w_router[l // 2].T, ((0, LANES - N_EXPERTS), (0, 0))))
            xt, h, meta, rows, counts = _xattn(*attn, router=(row(norm_ffn[l]), wr_pad))
            xt = _moe_layer(xt, h, meta, rows, counts,
                            moe_gu.reshape(N_EXPERTS, D_MODEL, 2 * D_FF_EXPERT),
                            moe_down.reshape(N_EXPERTS, D_FF_EXPERT, D_MODEL), row(norm_final))
    return xt.reshape(BATCH, SEQ, D_MODEL)
```

```python
import functools

import jax
import jax.numpy as jnp
from jax import lax
from jax.experimental import pallas as pl
from jax.experimental.pallas import tpu as pltpu

D_MODEL = 1024
BATCH = 8
SEQ = 4096
DEPTH = 2
MEM_LEN = 256
POOL_WIDTH = 512
POOL_GROUPS = 4
POOL_WINDOWS = (2, 4, 8, 16)
POOL_GROUP_DIM = POOL_WIDTH // POOL_GROUPS
SGU_WIDTH = 512
SGU_GROUPS = 4
SGU_GROUP_DIM = SGU_WIDTH // SGU_GROUPS
CHUNK = 128
IN_COLS = POOL_WIDTH + 2 * SGU_WIDTH + 2 * D_MODEL
XATTN_HEADS = 4
XATTN_HEAD_DIM = D_MODEL // XATTN_HEADS
D_FF = 2816
N_EXPERTS = 8
TOP_K = 2
D_FF_EXPERT = 3584
EPS = 1e-6

N_TOKENS = BATCH * SEQ
LANES = 128
POOL_HALO = 16
TOK_TILE = 512
MIXER_SLAB = 512
ROW_TILE = 512
SUBLANES = 8
N_BLOCKS = N_TOKENS // TOK_TILE
BLOCK_ROWS = -(-(TOK_TILE * TOP_K + N_EXPERTS * (SUBLANES - 1)) // LANES) * LANES
MAX_SORTED = N_TOKENS * TOP_K + N_BLOCKS * N_EXPERTS * (SUBLANES - 1)
N_ROW_TILES = -(-MAX_SORTED // ROW_TILE) + N_EXPERTS
N_SORTED_ROWS = N_ROW_TILES * ROW_TILE
FF_CHUNKS = 2
EXPERT_FF_CHUNKS = 2
VMEM_LIMIT = 52 * 1024 * 1024

BF16 = jnp.bfloat16
F32 = jnp.float32


def _dot(a, b):
    return jnp.dot(a, b, preferred_element_type=F32)


def _rms(x, g):
    return x * lax.rsqrt(jnp.mean(x * x, axis=-1, keepdims=True) + EPS) * g


def _const_spec(shape):
    return pl.BlockSpec(shape, lambda *_: (0,) * len(shape), pipeline_mode=pl.Buffered(1))


def _params(n_axes=1):
    return pltpu.CompilerParams(dimension_semantics=("arbitrary",) * n_axes,
                                vmem_limit_bytes=VMEM_LIMIT)


def _kv_kernel(mem_ref, g_ref, w_ref, o_ref):
    mem_n = _rms(mem_ref[...], g_ref[...]).astype(BF16)
    o_ref[...] = _dot(mem_n, w_ref[...]).astype(BF16)


def _kv(mem2d, norm_mem, w_xkv):
    return pl.pallas_call(
        _kv_kernel,
        grid=(DEPTH, BATCH),
        in_specs=[
            pl.BlockSpec((MEM_LEN, D_MODEL), lambda l, b: (b, 0)),
            pl.BlockSpec((1, D_MODEL), lambda l, b: (0, 0)),
            pl.BlockSpec((None, D_MODEL, 2 * D_MODEL), lambda l, b: (l, 0, 0)),
        ],
        out_specs=pl.BlockSpec((None, MEM_LEN, 2 * D_MODEL), lambda l, b: (l, b, 0)),
        out_shape=jax.ShapeDtypeStruct((DEPTH, BATCH * MEM_LEN, 2 * D_MODEL), BF16),
        compiler_params=_params(2),
        name="kv",
    )(mem2d, norm_mem, w_xkv)


def _mixer_kernel(x_ref, nm_ref, win_ref, pmix_ref, pscale_ref, lng_ref, lnb_ref, wsp_ref,
                  bsp_ref, wa_ref, wb_ref, wout_ref, o_ref, pext_ref, ain_ref, vn_ref, sg_ref):
    T = TOK_TILE
    R = MIXER_SLAB
    G = POOL_GROUP_DIM
    tile_in_seq = pl.program_id(0) % (SEQ // T)

    @pl.when(tile_in_seq == 0)
    def _():
        pext_ref[0:POOL_HALO, :] = jnp.zeros((POOL_HALO, POOL_WIDTH), F32)

    causal = (lax.broadcasted_iota(jnp.int32, (CHUNK, CHUNK), 0)
              >= lax.broadcasted_iota(jnp.int32, (CHUNK, CHUNK), 1))
    w_sp = [jnp.where(causal, wsp_ref[gi], 0.0).astype(BF16) for gi in range(SGU_GROUPS)]

    for r0 in range(0, T, R):
        rows = slice(r0, r0 + R)
        x = x_ref[rows, :]
        h = _rms(x, nm_ref[...]).astype(BF16)

        c0 = POOL_WIDTH + 2 * SGU_WIDTH
        p = _dot(h, win_ref[:, 0:POOL_WIDTH])
        u = _dot(h, win_ref[:, POOL_WIDTH:POOL_WIDTH + SGU_WIDTH])
        v = _dot(h, win_ref[:, POOL_WIDTH + SGU_WIDTH:c0])
        ga = _dot(h, win_ref[:, c0:c0 + D_MODEL])
        gb = _dot(h, win_ref[:, c0 + D_MODEL:c0 + 2 * D_MODEL])
        gate_a = jax.nn.sigmoid(ga)
        gate_b = jax.nn.sigmoid(gb)

        pext_ref[POOL_HALO + r0:POOL_HALO + r0 + R, :] = p
        t_pos = tile_in_seq * T + r0 + lax.broadcasted_iota(jnp.int32, (R, G), 0)
        for gi, w in enumerate(POOL_WINDOWS):
            cols = slice(gi * G, (gi + 1) * G)
            pg = p[:, cols]
            acc = pg
            for j in range(1, w):
                acc = acc + pext_ref[POOL_HALO + r0 - j:POOL_HALO + r0 - j + R, cols]
            cnt = jnp.minimum(t_pos + 1, w).astype(F32)
            pooled = acc / cnt - pg
            mixed = _dot(pooled.astype(BF16), pmix_ref[gi])
            ain_ref[rows, cols] = (mixed * pscale_ref[:, cols]).astype(BF16)
        br_a = _dot(ain_ref[rows, :], wa_ref[...])

        u = jax.nn.gelu(u)
        v = jax.nn.gelu(v)
        mu = jnp.mean(v, axis=-1, keepdims=True)
        vc = v - mu
        vn = (vc * lax.rsqrt(jnp.mean(vc * vc, axis=-1, keepdims=True) + EPS) * lng_ref[...]
              + lnb_ref[...])
        vn_ref[rows, :] = vn.astype(BF16)
        n_chunks = R // CHUNK
        for gi in range(SGU_GROUPS):
            cols = slice(gi * SGU_GROUP_DIM, (gi + 1) * SGU_GROUP_DIM)
            rhs = jnp.concatenate([vn_ref[r0 + c * CHUNK:r0 + (c + 1) * CHUNK, cols]
                                   for c in range(n_chunks)], axis=1)
            mixed = _dot(w_sp[gi], rhs)
            for c in range(n_chunks):
                crow = slice(c * CHUNK, (c + 1) * CHUNK)
                blk = mixed[:, c * SGU_GROUP_DIM:(c + 1) * SGU_GROUP_DIM] + bsp_ref[:, cols]
                sg_ref[r0 + c * CHUNK:r0 + (c + 1) * CHUNK, cols] = (u[crow, cols] * blk).astype(BF16)
        br_b = _dot(sg_ref[rows, :], wb_ref[...])

        merged = gate_a * br_a + gate_b * br_b
        o_ref[rows, :] = x + _dot(merged.astype(BF16), wout_ref[...])

    pext_ref[0:POOL_HALO, :] = pext_ref[T:T + POOL_HALO, :]


def _mixer(x, nm, win, pmix, pscale, lng, lnb, wsp, bsp_tile, wa, wb, wout):
    T = TOK_TILE
    tok = pl.BlockSpec((T, D_MODEL), lambda i: (i, 0))
    return pl.pallas_call(
        _mixer_kernel,
        grid=(N_TOKENS // T,),
        in_specs=[
            tok,
            _const_spec((1, D_MODEL)),
            _const_spec((D_MODEL, IN_COLS)),
            _const_spec((POOL_GROUPS, POOL_GROUP_DIM, POOL_GROUP_DIM)),
            _const_spec((1, POOL_WIDTH)),
            _const_spec((1, SGU_WIDTH)),
            _const_spec((1, SGU_WIDTH)),
            _const_spec((SGU_GROUPS, CHUNK, CHUNK)),
            _const_spec((CHUNK, SGU_WIDTH)),
            _const_spec((POOL_WIDTH, D_MODEL)),
            _const_spec((SGU_WIDTH, D_MODEL)),
            _const_spec((D_MODEL, D_MODEL)),
        ],
        out_specs=tok,
        out_shape=jax.ShapeDtypeStruct((N_TOKENS, D_MODEL), F32),
        scratch_shapes=[
            pltpu.VMEM((POOL_HALO + T, POOL_WIDTH), F32),
            pltpu.VMEM((T, POOL_WIDTH), BF16),
            pltpu.VMEM((T, SGU_WIDTH), BF16),
            pltpu.VMEM((T, SGU_WIDTH), BF16),
        ],
        compiler_params=_params(),
        name="mixer",
    )(x, nm, win, pmix, pscale, lng, lnb, wsp, bsp_tile, wa, wb, wout)


def _xattn_kernel(x_ref, g_ref, wq_ref, kv_ref, wo_ref, *rest, route):
    if route:
        gf_ref, wr_ref, tri_ref, o_ref, h_ref, meta_ref, rows_ref, cnt_ref, ocat_ref = rest
    else:
        o_ref, ocat_ref = rest
    x = x_ref[...]
    h = _rms(x, g_ref[...]).astype(BF16)
    q = (_dot(h, wq_ref[...]) * (XATTN_HEAD_DIM ** -0.5)).astype(BF16)
    head_cols = [slice(hd * XATTN_HEAD_DIM, (hd + 1) * XATTN_HEAD_DIM) for hd in range(XATTN_HEADS)]
    scores = [lax.dot_general(q[:, cols], kv_ref[:, cols], (((1,), (1,)), ((), ())),
                              preferred_element_type=F32) for cols in head_cols]
    for hd, cols in enumerate(head_cols):
        v = kv_ref[:, D_MODEL + hd * XATTN_HEAD_DIM:D_MODEL + (hd + 1) * XATTN_HEAD_DIM]
        s = scores[hd]
        e = jnp.exp(s - jnp.max(s, axis=-1, keepdims=True))
        pr = e / jnp.sum(e, axis=-1, keepdims=True)
        ocat_ref[:, cols] = _dot(pr.astype(BF16), v).astype(BF16)
    x = x + _dot(ocat_ref[...], wo_ref[...])
    o_ref[...] = x
    if route:
        _route_rows(x, gf_ref, wr_ref, tri_ref, h_ref, meta_ref, rows_ref, cnt_ref)


def _xattn(x, g, wq, kv_l, wo, router=None):
    T = TOK_TILE
    tok = pl.BlockSpec((T, D_MODEL), lambda i: (i, 0))
    slab = pl.BlockSpec((T, LANES), lambda i: (i, 0))
    in_specs = [
        tok,
        _const_spec((1, D_MODEL)),
        _const_spec((D_MODEL, D_MODEL)),
        pl.BlockSpec((MEM_LEN, 2 * D_MODEL), lambda i: (i // (SEQ // T), 0)),
        _const_spec((D_MODEL, D_MODEL)),
    ]
    out_specs = [tok]
    out_shape = [jax.ShapeDtypeStruct((N_TOKENS, D_MODEL), F32)]
    args = (x, g, wq, kv_l, wo)
    if router is not None:
        in_specs += [_const_spec((1, D_MODEL)), _const_spec((LANES, D_MODEL)), _const_spec((T, T))]
        out_specs += [tok, slab, pl.BlockSpec((None, N_EXPERTS, T), lambda i: (i, 0, 0)),
                      pl.BlockSpec((None, N_EXPERTS, LANES), lambda i: (i, 0, 0))]
        out_shape += [
            jax.ShapeDtypeStruct((N_TOKENS, D_MODEL), BF16),
            jax.ShapeDtypeStruct((N_TOKENS, LANES), jnp.int32),
            jax.ShapeDtypeStruct((N_BLOCKS, N_EXPERTS, T), F32),
            jax.ShapeDtypeStruct((N_BLOCKS, N_EXPERTS, LANES), jnp.int32),
        ]
        args += tuple(router)
    return pl.pallas_call(
        functools.partial(_xattn_kernel, route=router is not None),
        grid=(N_BLOCKS,),
        in_specs=in_specs,
        out_specs=out_specs,
        out_shape=out_shape,
        scratch_shapes=[pltpu.VMEM((T, D_MODEL), BF16)],
        compiler_params=_params(),
        name="xattn",
    )(*args)


def _swiglu_rows(h, wgu_ref, wd_ref, d_ff, n_chunks):
    fc = d_ff // n_chunks
    acc = None
    for c in range(n_chunks):
        g = _dot(h, wgu_ref[:, c * fc:(c + 1) * fc])
        up = _dot(h, wgu_ref[:, d_ff + c * fc:d_ff + (c + 1) * fc])
        a = (g * jax.nn.sigmoid(g) * up).astype(BF16)
        part = _dot(a, wd_ref[c * fc:(c + 1) * fc, :])
        acc = part if acc is None else acc + part
    return acc


def _ffn_kernel(x_ref, g_ref, wgu_ref, wd_ref, *rest):
    n_cast = (len(rest) - 1) // 2
    cast_in, o_ref, cast_out = rest[:n_cast], rest[n_cast], rest[n_cast + 1:]
    x = x_ref[...]
    h = _rms(x, g_ref[...]).astype(BF16)
    o_ref[...] = x + _swiglu_rows(h, wgu_ref, wd_ref, D_FF, FF_CHUNKS)
    for src, dst in zip(cast_in, cast_out):
        dst[...] = src[...].astype(BF16)


def _ffn(x, g, wgu, wd, cast_weights=()):
    T = TOK_TILE
    steps = N_TOKENS // T
    tok = pl.BlockSpec((T, D_MODEL), lambda i: (i, 0))
    cast_in_specs, cast_specs, cast_shapes = [], [], []
    for w, first_row, n_rows in cast_weights:
        rows = n_rows // steps
        assert rows * steps == n_rows and rows % (2 * SUBLANES) == 0 and first_row % rows == 0
        cast_in_specs.append(pl.BlockSpec((rows, w.shape[1]),
                                          lambda i, first=first_row // rows: (i + first, 0)))
        cast_specs.append(pl.BlockSpec((rows, w.shape[1]), lambda i: (i, 0)))
        cast_shapes.append(jax.ShapeDtypeStruct((n_rows, w.shape[1]), BF16))
    cast_weights = [w for w, _, _ in cast_weights]
    return pl.pallas_call(
        _ffn_kernel,
        grid=(steps,),
        in_specs=[tok, _const_spec((1, D_MODEL)), _const_spec((D_MODEL, 2 * D_FF)),
                  _const_spec((D_FF, D_MODEL))] + cast_in_specs,
        out_specs=[tok] + cast_specs,
        out_shape=[jax.ShapeDtypeStruct((N_TOKENS, D_MODEL), F32)]
                  + cast_shapes,
        compiler_params=_params(),
        name="ffn",
    )(x, g, wgu, wd, *cast_weights)


def _route_rows(x, g_ref, wr_ref, tri_ref, h_ref, meta_ref, rows_ref, cnt_ref):
    T = TOK_TILE
    E = N_EXPERTS
    assert E == SUBLANES
    h = _rms(x, g_ref[...]).astype(BF16)
    h_ref[...] = h
    logits = lax.dot_general(wr_ref[...], h, (((1,), (1,)), ((), ())),
                             preferred_element_type=F32)[0:E]
    ex = lax.broadcasted_iota(jnp.int32, (E, T), 0)
    m1 = jnp.max(logits, axis=0, keepdims=True)
    i1 = jnp.min(jnp.where(logits == m1, ex, E), axis=0, keepdims=True)
    rest = jnp.where(ex == i1, -jnp.inf, logits)
    m2 = jnp.max(rest, axis=0, keepdims=True)
    i2 = jnp.min(jnp.where(rest == m2, ex, E), axis=0, keepdims=True)
    e2 = jnp.exp(m2 - m1)
    g1 = 1.0 / (1.0 + e2)
    g2 = e2 / (1.0 + e2)

    sel1 = ex == i1
    sel2 = ex == i2
    onehot = jnp.where(sel1, 1.0, jnp.where(sel2, 1.0, 0.0))
    onehot_mxu = jnp.concatenate([onehot, jnp.zeros((E, T), F32)], axis=0).astype(BF16)
    before = _dot(onehot_mxu, tri_ref[...])[0:E]
    r1 = jnp.sum(jnp.where(sel1, before, 0.0), axis=0, keepdims=True)
    r2 = jnp.sum(jnp.where(sel2, before, 0.0), axis=0, keepdims=True)
    cnt_ref[...] = jnp.broadcast_to(jnp.sum(onehot, axis=1, keepdims=True), (E, LANES)).astype(jnp.int32)

    rows = jnp.where(ex == 0, i1.astype(F32), jnp.where(ex == 1, i2.astype(F32), jnp.where(
        ex == 2, r1, jnp.where(ex == 3, r2, jnp.where(ex == 4, g1, jnp.where(ex == 5, g2, 0.0))))))
    cols = jnp.concatenate([rows, jnp.zeros((LANES - E, T), F32)], axis=0).T
    meta_ref[...] = cols.astype(jnp.int32)
    rows_ref[...] = rows


def _segment_copies(src, src_row, dst, dst_row, n_rows, max_rows, sem, action):
    size = max_rows
    while size >= SUBLANES:
        done = pl.multiple_of((n_rows // (2 * size)) * (2 * size), SUBLANES)

        @pl.when((n_rows & size) != 0)
        def _(size=size, done=done):
            src_at = 0 if src_row is None else src_row + done
            copy = pltpu.make_async_copy(src.at[pl.ds(src_at, size)],
                                         dst.at[pl.ds(dst_row + done, size)], sem)
            copy.start() if action == "start" else copy.wait()

        size //= 2


def _block_positions_rows(rows_ref, lstart_s, b):
    pos = []
    for k in range(TOP_K):
        expert = rows_ref[k:k + 1, :].astype(jnp.int32)
        start = jnp.zeros_like(expert)
        for e in range(N_EXPERTS):
            start = jnp.where(expert == e, lstart_s[b * N_EXPERTS + e], start)
        pos.append(start + rows_ref[TOP_K + k:TOP_K + k + 1, :].astype(jnp.int32))
    return pos


def _block_positions(meta_ref, lstart_ref):
    lane = lax.broadcasted_iota(jnp.int32, (TOK_TILE, LANES), 1)
    meta = meta_ref[...]
    lstart = lstart_ref[...]
    pos = []
    for k in range(TOP_K):
        start = jnp.sum(jnp.where(lane == meta[:, k:k + 1], lstart, 0), axis=-1, keepdims=True)
        pos.append(start + meta[:, TOP_K + k:TOP_K + k + 1])
    return pos


def _dispatch_kernel(lstart_s, goff_s, pcnt_s, tail_off_s, tail_len_s, rows_ref, h_ref,
                     xs_ref, blk_ref, zero_ref, sem, fill_sem):
    b = pl.program_id(0)

    @pl.when(b == 0)
    def _():
        zero_ref[...] = jnp.zeros(zero_ref.shape, F32)
        fill_rows = zero_ref.shape[0]

        def unused_tiles(action):
            def body(i, carry):
                copy = pltpu.make_async_copy(
                    zero_ref, xs_ref.at[pl.ds(pl.multiple_of(i * fill_rows, SUBLANES), fill_rows)],
                    fill_sem)
                copy.start() if action == "start" else copy.wait()
                return carry
            lax.fori_loop(tail_off_s[N_EXPERTS] * (ROW_TILE // fill_rows),
                          N_ROW_TILES * (ROW_TILE // fill_rows), body, 0)

        for action in ("start", "wait"):
            for e in range(N_EXPERTS):
                _segment_copies(zero_ref, None, xs_ref, pl.multiple_of(tail_off_s[e], SUBLANES),
                                tail_len_s[e], fill_rows, fill_sem, action)
            unused_tiles(action)

    def segments(blk, slot, action):
        for e in range(N_EXPERTS):
            _segment_copies(blk_ref.at[slot], pl.multiple_of(lstart_s[blk * N_EXPERTS + e], SUBLANES),
                            xs_ref, pl.multiple_of(goff_s[blk * N_EXPERTS + e], SUBLANES),
                            pcnt_s[blk * N_EXPERTS + e], TOK_TILE, sem.at[slot], action)

    slot = b % 2
    p0, p1 = _block_positions_rows(rows_ref, lstart_s, b)
    r = lax.broadcasted_iota(jnp.int32, (BLOCK_ROWS, TOK_TILE), 0)
    sel = jnp.where(r == p0, 1.0, jnp.where(r == p1, 1.0, 0.0)).astype(BF16)
    blk_ref[slot] = _dot(sel, h_ref[...])
    segments(b, slot, "start")

    @pl.when(b > 0)
    def _():
        segments(b - 1, 1 - slot, "wait")

    @pl.when(b == N_BLOCKS - 1)
    def _():
        segments(b, slot, "wait")


def _dispatch(lstart, goff, pcnt, tail_off, tail_len, rows, h):
    T = TOK_TILE
    grid_spec = pltpu.PrefetchScalarGridSpec(
        num_scalar_prefetch=5,
        grid=(N_BLOCKS,),
        in_specs=[
            pl.BlockSpec((None, N_EXPERTS, T), lambda i, *_: (i, 0, 0)),
            pl.BlockSpec((T, D_MODEL), lambda i, *_: (i, 0)),
        ],
        out_specs=pl.BlockSpec(memory_space=pl.ANY),
        scratch_shapes=[pltpu.VMEM((2, BLOCK_ROWS, D_MODEL), F32),
                        pltpu.VMEM((ROW_TILE // 2, D_MODEL), F32),
                        pltpu.SemaphoreType.DMA((2,)),
                        pltpu.SemaphoreType.DMA(())],
    )
    return pl.pallas_call(
        _dispatch_kernel,
        grid_spec=grid_spec,
        out_shape=jax.ShapeDtypeStruct((N_SORTED_ROWS, D_MODEL), F32),
        compiler_params=_params(),
        name="dispatch",
    )(lstart, goff, pcnt, tail_off, tail_len, rows, h)


def _experts_kernel(tile_e_ref, tile_src_ref, xs_ref, wgu_ref, wd_ref, y_ref):
    del tile_e_ref
    on = tile_src_ref[pl.program_id(0)] == pl.program_id(0)

    @pl.when(on)
    def _():
        y_ref[...] = _swiglu_rows(xs_ref[...].astype(BF16), wgu_ref, wd_ref, D_FF_EXPERT,
                                  EXPERT_FF_CHUNKS)

    @pl.when(jnp.logical_not(on))
    def _():
        y_ref[...] = jnp.zeros((ROW_TILE, D_MODEL), F32)


def _experts(tile_e, tile_src, xs, wgu, wd):
    grid_spec = pltpu.PrefetchScalarGridSpec(
        num_scalar_prefetch=2,
        grid=(N_ROW_TILES,),
        in_specs=[
            pl.BlockSpec((ROW_TILE, D_MODEL), lambda i, te, src: (src[i], 0)),
            pl.BlockSpec((None, D_MODEL, 2 * D_FF_EXPERT), lambda i, te, src: (te[i], 0, 0),
                         pipeline_mode=pl.Buffered(1)),
            pl.BlockSpec((None, D_FF_EXPERT, D_MODEL), lambda i, te, src: (te[i], 0, 0)),
        ],
        out_specs=pl.BlockSpec((ROW_TILE, D_MODEL), lambda i, te, src: (i, 0)),
    )
    return pl.pallas_call(
        _experts_kernel,
        grid_spec=grid_spec,
        out_shape=jax.ShapeDtypeStruct((N_SORTED_ROWS, D_MODEL), F32),
        compiler_params=_params(),
        name="experts",
    )(tile_e, tile_src, xs, wgu, wd)


def _combine_kernel(lstart_s, goff_s, pcnt_s, meta_ref, lstart_ref, rows_ref, x_ref, nf_ref, y_ref,
                    o_ref, buf_ref, sem):
    b = pl.program_id(0)
    slot = b % 2

    def fetch(blk, slot, action):
        for e in range(N_EXPERTS):
            _segment_copies(y_ref, pl.multiple_of(goff_s[blk * N_EXPERTS + e], SUBLANES),
                            buf_ref.at[slot], pl.multiple_of(lstart_s[blk * N_EXPERTS + e], SUBLANES),
                            pcnt_s[blk * N_EXPERTS + e], TOK_TILE, sem.at[slot], action)

    def prefetch(blk, slot):
        buf_ref[slot] = jnp.zeros(buf_ref.shape[1:], F32)
        fetch(blk, slot, "start")

    @pl.when(b == 0)
    def _():
        prefetch(0, 0)

    @pl.when(b + 1 < N_BLOCKS)
    def _():
        prefetch(b + 1, 1 - slot)

    fetch(b, slot, "wait")

    q0, q1 = _block_positions_rows(rows_ref, lstart_s, b)
    rr = lax.broadcasted_iota(jnp.int32, (BLOCK_ROWS, TOK_TILE), 0)
    row_gate = jnp.sum(jnp.where(rr == q0, rows_ref[2 * TOP_K:2 * TOP_K + 1, :],
                                 jnp.where(rr == q1, rows_ref[2 * TOP_K + 1:2 * TOP_K + 2, :], 0.0)),
                       axis=1, keepdims=True)
    y = (buf_ref[slot] * row_gate).astype(BF16)

    p0, p1 = _block_positions(meta_ref, lstart_ref)
    r = lax.broadcasted_iota(jnp.int32, (TOK_TILE, BLOCK_ROWS), 1)
    pick = jnp.where(r == p0, 1.0, jnp.where(r == p1, 1.0, 0.0)).astype(BF16)
    o_ref[...] = _rms(x_ref[...] + _dot(pick, y), nf_ref[...])


def _combine(lstart, goff, pcnt, meta, lstart_rows, rows, x, norm_final, y):
    T = TOK_TILE
    grid_spec = pltpu.PrefetchScalarGridSpec(
        num_scalar_prefetch=3,
        grid=(N_BLOCKS,),
        in_specs=[
            pl.BlockSpec((T, LANES), lambda i, *_: (i, 0)),
            pl.BlockSpec((None, 1, LANES), lambda i, *_: (i, 0, 0)),
            pl.BlockSpec((None, N_EXPERTS, T), lambda i, *_: (i, 0, 0)),
            pl.BlockSpec((T, D_MODEL), lambda i, *_: (i, 0)),
            pl.BlockSpec((1, D_MODEL), lambda i, *_: (0, 0)),
            pl.BlockSpec(memory_space=pl.ANY),
        ],
        out_specs=pl.BlockSpec((T, D_MODEL), lambda i, *_: (i, 0)),
        scratch_shapes=[pltpu.VMEM((2, BLOCK_ROWS, D_MODEL), F32), pltpu.SemaphoreType.DMA((2,))],
    )
    return pl.pallas_call(
        _combine_kernel,
        grid_spec=grid_spec,
        out_shape=jax.ShapeDtypeStruct((N_TOKENS, D_MODEL), F32),
        compiler_params=_params(),
        name="combine",
    )(lstart, goff, pcnt, meta, lstart_rows, rows, x, norm_final, y)


def _moe_layer(x, h, meta, rows, counts, w_gu, w_down, norm_final):
    i32 = lambda a: a.astype(jnp.int32)
    pcnt = -(-counts[:, :, 0] // SUBLANES) * SUBLANES
    lstart = jnp.cumsum(pcnt, axis=1) - pcnt
    rows_e = jnp.sum(pcnt, axis=0)
    tiles_e = -(-rows_e // ROW_TILE)
    tile_end = jnp.cumsum(tiles_e)
    region = (tile_end - tiles_e) * ROW_TILE
    goff = region[None, :] + jnp.cumsum(pcnt, axis=0) - pcnt
    tail_off = jnp.concatenate([region + rows_e, tile_end[-1:]])
    tail_len = tiles_e * ROW_TILE - rows_e
    tile_ids = jnp.arange(N_ROW_TILES, dtype=jnp.int32)
    tile_src = jnp.minimum(tile_ids, jnp.maximum(tile_end[-1] - 1, 0))
    tile_e = jnp.minimum(jnp.sum(tile_src[:, None] >= tile_end[None, :], axis=1), N_EXPERTS - 1)
    lstart_rows = jnp.pad(lstart, ((0, 0), (0, LANES - N_EXPERTS)))[:, None, :]
    flat = lambda a: i32(a).reshape(-1)

    xs = _dispatch(flat(lstart), flat(goff), flat(pcnt), i32(tail_off), i32(tail_len), rows, h)
    y = _experts(i32(tile_e), i32(tile_src), xs, w_gu, w_down)
    return _combine(flat(lstart), flat(goff), flat(pcnt), meta, i32(lstart_rows), rows, x,
                    norm_final, y)


def kernel(x, mem, norm_mem, norm_mix, w_in, pool_mix, pool_scale, sgu_ln_g, sgu_ln_b, w_spatial, b_spatial, w_branch_a, w_branch_b, w_out, norm_xattn, w_xq, w_xkv, w_xo, norm_ffn, w_ff_gu, w_ff_down, w_router, w_moe_gu, w_moe_down, norm_final):
    assert DEPTH == 2 and x.shape == (BATCH, SEQ, D_MODEL)
    row = lambda a: a.reshape(1, -1)
    bf = lambda a: a.astype(BF16)

    xt = x.reshape(N_TOKENS, D_MODEL)
    kv = _kv(mem.reshape(BATCH * MEM_LEN, D_MODEL), row(norm_mem), bf(w_xkv))
    w_in_b, w_out_b, w_xq_b, w_xo_b = bf(w_in[0]), bf(w_out[0]), bf(w_xq[0]), bf(w_xo[0])
    for l in range(DEPTH):
        bsp_tile = jnp.repeat(b_spatial[l].T, SGU_GROUP_DIM, axis=1)
        xt = _mixer(xt, row(norm_mix[l]), w_in_b, bf(pool_mix[l]), row(pool_scale[l]),
                    row(sgu_ln_g[l]), row(sgu_ln_b[l]), w_spatial[l], bsp_tile,
                    bf(w_branch_a[l]), bf(w_branch_b[l]), w_out_b)
        attn = (xt, row(norm_xattn[l]), w_xq_b, kv[l], w_xo_b)
        if l % 2 == 0:
            (xt,) = _xattn(*attn)
            m = (l + 1) // 2
            stacked = lambda w: (w.reshape(DEPTH * D_MODEL, w.shape[-1]), (l + 1) * D_MODEL, D_MODEL)
            xt, moe_gu, moe_down, w_in_b, w_out_b, w_xq_b, w_xo_b = _ffn(
                xt, row(norm_ffn[l]), bf(w_ff_gu[l // 2]), bf(w_ff_down[l // 2]),
                cast_weights=(
                    (w_moe_gu[m].reshape(N_EXPERTS * D_MODEL, 2 * D_FF_EXPERT), 0, N_EXPERTS * D_MODEL),
                    (w_moe_down[m].reshape(N_EXPERTS * D_FF_EXPERT, D_MODEL), 0, N_EXPERTS * D_FF_EXPERT),
                    stacked(w_in), stacked(w_out), stacked(w_xq), stacked(w_xo)))
        else:
            wr_pad = bf(jnp.pad(w_router[l // 2].T, ((0, LANES - N_EXPERTS), (0, 0))))
            token = jnp.arange(TOK_TILE)
            precedes = bf(token[:, None] < token[None, :])
            xt, h, meta, rows, counts = _xattn(*attn, router=(row(norm_ffn[l]), wr_pad, precedes))
            xt = _moe_layer(xt, h, meta, rows, counts,
                            moe_gu.reshape(N_EXPERTS, D_MODEL, 2 * D_FF_EXPERT),
                            moe_down.reshape(N_EXPERTS, D_FF_EXPERT, D_MODEL), row(norm_final))
    return xt.reshape(BATCH, SEQ, D_MODEL)
```

```python
import functools

import jax
import jax.numpy as jnp
from jax import lax
from jax.experimental import pallas as pl
from jax.experimental.pallas import tpu as pltpu

D_MODEL = 1024
BATCH = 8
SEQ = 4096
DEPTH = 2
MEM_LEN = 256
POOL_WIDTH = 512
POOL_GROUPS = 4
POOL_WINDOWS = (2, 4, 8, 16)
POOL_GROUP_DIM = POOL_WIDTH // POOL_GROUPS
SGU_WIDTH = 512
SGU_GROUPS = 4
SGU_GROUP_DIM = SGU_WIDTH // SGU_GROUPS
CHUNK = 128
IN_COLS = POOL_WIDTH + 2 * SGU_WIDTH + 2 * D_MODEL
XATTN_HEADS = 4
XATTN_HEAD_DIM = D_MODEL // XATTN_HEADS
D_FF = 2816
N_EXPERTS = 8
TOP_K = 2
D_FF_EXPERT = 3584
EPS = 1e-6

N_TOKENS = BATCH * SEQ
LANES = 128
POOL_HALO = 16
TOK_TILE = 512
MIXER_SLAB = 512
ROW_TILE = 512
SUBLANES = 8
N_BLOCKS = N_TOKENS // TOK_TILE
BLOCK_ROWS = -(-(TOK_TILE * TOP_K + N_EXPERTS * (SUBLANES - 1)) // LANES) * LANES
MAX_SORTED = N_TOKENS * TOP_K + N_BLOCKS * N_EXPERTS * (SUBLANES - 1)
N_ROW_TILES = -(-MAX_SORTED // ROW_TILE) + N_EXPERTS
N_SORTED_ROWS = N_ROW_TILES * ROW_TILE
FF_CHUNKS = 1
EXPERT_FF_CHUNKS = 2
VMEM_LIMIT = 52 * 1024 * 1024

BF16 = jnp.bfloat16
F32 = jnp.float32


def _dot(a, b):
    return jnp.dot(a, b, preferred_element_type=F32)


def _rms(x, g):
    return x * lax.rsqrt(jnp.mean(x * x, axis=-1, keepdims=True) + EPS) * g


def _const_spec(shape):
    return pl.BlockSpec(shape, lambda *_: (0,) * len(shape), pipeline_mode=pl.Buffered(1))


def _params(n_axes=1):
    return pltpu.CompilerParams(dimension_semantics=("arbitrary",) * n_axes,
                                vmem_limit_bytes=VMEM_LIMIT)


def _kv_kernel(mem_ref, g_ref, w_ref, o_ref):
    mem_n = _rms(mem_ref[...], g_ref[...]).astype(BF16)
    o_ref[...] = _dot(mem_n, w_ref[...]).astype(BF16)


def _kv(mem2d, norm_mem, w_xkv):
    return pl.pallas_call(
        _kv_kernel,
        grid=(DEPTH, BATCH),
        in_specs=[
            pl.BlockSpec((MEM_LEN, D_MODEL), lambda l, b: (b, 0)),
            pl.BlockSpec((1, D_MODEL), lambda l, b: (0, 0)),
            pl.BlockSpec((None, D_MODEL, 2 * D_MODEL), lambda l, b: (l, 0, 0)),
        ],
        out_specs=pl.BlockSpec((None, MEM_LEN, 2 * D_MODEL), lambda l, b: (l, b, 0)),
        out_shape=jax.ShapeDtypeStruct((DEPTH, BATCH * MEM_LEN, 2 * D_MODEL), BF16),
        compiler_params=_params(2),
        name="kv",
    )(mem2d, norm_mem, w_xkv)


def _mixer_kernel(x_ref, nm_ref, win_ref, pmix_ref, pscale_ref, lng_ref, lnb_ref, wsp_ref,
                  bsp_ref, wa_ref, wb_ref, wout_ref, o_ref, pext_ref, ain_ref, vn_ref, sg_ref):
    T = TOK_TILE
    R = MIXER_SLAB
    G = POOL_GROUP_DIM
    tile_in_seq = pl.program_id(0) % (SEQ // T)

    @pl.when(tile_in_seq == 0)
    def _():
        pext_ref[0:POOL_HALO, :] = jnp.zeros((POOL_HALO, POOL_WIDTH), F32)

    causal = (lax.broadcasted_iota(jnp.int32, (CHUNK, CHUNK), 0)
              >= lax.broadcasted_iota(jnp.int32, (CHUNK, CHUNK), 1))
    w_sp = [jnp.where(causal, wsp_ref[gi], 0.0).astype(BF16) for gi in range(SGU_GROUPS)]

    for r0 in range(0, T, R):
        rows = slice(r0, r0 + R)
        x = x_ref[rows, :]
        h = _rms(x, nm_ref[...]).astype(BF16)

        c0 = POOL_WIDTH + 2 * SGU_WIDTH
        p = _dot(h, win_ref[:, 0:POOL_WIDTH])
        u = _dot(h, win_ref[:, POOL_WIDTH:POOL_WIDTH + SGU_WIDTH])
        v = _dot(h, win_ref[:, POOL_WIDTH + SGU_WIDTH:c0])
        ga = _dot(h, win_ref[:, c0:c0 + D_MODEL])
        gb = _dot(h, win_ref[:, c0 + D_MODEL:c0 + 2 * D_MODEL])
        gate_a = jax.nn.sigmoid(ga)
        gate_b = jax.nn.sigmoid(gb)

        pext_ref[POOL_HALO + r0:POOL_HALO + r0 + R, :] = p
        t_pos = tile_in_seq * T + r0 + lax.broadcasted_iota(jnp.int32, (R, G), 0)
        for gi, w in enumerate(POOL_WINDOWS):
            cols = slice(gi * G, (gi + 1) * G)
            pg = p[:, cols]
            acc = pg
            for j in range(1, w):
                acc = acc + pext_ref[POOL_HALO + r0 - j:POOL_HALO + r0 - j + R, cols]
            cnt = jnp.minimum(t_pos + 1, w).astype(F32)
            pooled = acc / cnt - pg
            mixed = _dot(pooled.astype(BF16), pmix_ref[gi])
            ain_ref[rows, cols] = (mixed * pscale_ref[:, cols]).astype(BF16)
        br_a = _dot(ain_ref[rows, :], wa_ref[...])

        u = jax.nn.gelu(u)
        v = jax.nn.gelu(v)
        mu = jnp.mean(v, axis=-1, keepdims=True)
        vc = v - mu
        vn = (vc * lax.rsqrt(jnp.mean(vc * vc, axis=-1, keepdims=True) + EPS) * lng_ref[...]
              + lnb_ref[...])
        vn_ref[rows, :] = vn.astype(BF16)
        n_chunks = R // CHUNK
        for gi in range(SGU_GROUPS):
            cols = slice(gi * SGU_GROUP_DIM, (gi + 1) * SGU_GROUP_DIM)
            rhs = jnp.concatenate([vn_ref[r0 + c * CHUNK:r0 + (c + 1) * CHUNK, cols]
                                   for c in range(n_chunks)], axis=1)
            mixed = _dot(w_sp[gi], rhs)
            for c in range(n_chunks):
                crow = slice(c * CHUNK, (c + 1) * CHUNK)
                blk = mixed[:, c * SGU_GROUP_DIM:(c + 1) * SGU_GROUP_DIM] + bsp_ref[:, cols]
                sg_ref[r0 + c * CHUNK:r0 + (c + 1) * CHUNK, cols] = (u[crow, cols] * blk).astype(BF16)
        br_b = _dot(sg_ref[rows, :], wb_ref[...])

        merged = gate_a * br_a + gate_b * br_b
        o_ref[rows, :] = x + _dot(merged.astype(BF16), wout_ref[...])

    pext_ref[0:POOL_HALO, :] = pext_ref[T:T + POOL_HALO, :]


def _mixer(x, nm, win, pmix, pscale, lng, lnb, wsp, bsp_tile, wa, wb, wout):
    T = TOK_TILE
    tok = pl.BlockSpec((T, D_MODEL), lambda i: (i, 0))
    return pl.pallas_call(
        _mixer_kernel,
        grid=(N_TOKENS // T,),
        in_specs=[
            tok,
            _const_spec((1, D_MODEL)),
            _const_spec((D_MODEL, IN_COLS)),
            _const_spec((POOL_GROUPS, POOL_GROUP_DIM, POOL_GROUP_DIM)),
            _const_spec((1, POOL_WIDTH)),
            _const_spec((1, SGU_WIDTH)),
            _const_spec((1, SGU_WIDTH)),
            _const_spec((SGU_GROUPS, CHUNK, CHUNK)),
            _const_spec((CHUNK, SGU_WIDTH)),
            _const_spec((POOL_WIDTH, D_MODEL)),
            _const_spec((SGU_WIDTH, D_MODEL)),
            _const_spec((D_MODEL, D_MODEL)),
        ],
        out_specs=tok,
        out_shape=jax.ShapeDtypeStruct((N_TOKENS, D_MODEL), F32),
        scratch_shapes=[
            pltpu.VMEM((POOL_HALO + T, POOL_WIDTH), F32),
            pltpu.VMEM((T, POOL_WIDTH), BF16),
            pltpu.VMEM((T, SGU_WIDTH), BF16),
            pltpu.VMEM((T, SGU_WIDTH), BF16),
        ],
        compiler_params=_params(),
        name="mixer",
    )(x, nm, win, pmix, pscale, lng, lnb, wsp, bsp_tile, wa, wb, wout)


def _xattn_kernel(x_ref, g_ref, wq_ref, kv_ref, wo_ref, *rest, route):
    if route:
        gf_ref, wr_ref, o_ref, h_ref, meta_ref, rows_ref, cnt_ref, ocat_ref = rest
    else:
        o_ref, ocat_ref = rest
    x = x_ref[...]
    h = _rms(x, g_ref[...]).astype(BF16)
    q = (_dot(h, wq_ref[...]) * (XATTN_HEAD_DIM ** -0.5)).astype(BF16)
    head_cols = [slice(hd * XATTN_HEAD_DIM, (hd + 1) * XATTN_HEAD_DIM) for hd in range(XATTN_HEADS)]
    scores = [lax.dot_general(q[:, cols], kv_ref[:, cols], (((1,), (1,)), ((), ())),
                              preferred_element_type=F32) for cols in head_cols]
    for hd, cols in enumerate(head_cols):
        v = kv_ref[:, D_MODEL + hd * XATTN_HEAD_DIM:D_MODEL + (hd + 1) * XATTN_HEAD_DIM]
        s = scores[hd]
        e = jnp.exp(s - jnp.max(s, axis=-1, keepdims=True))
        pr = e / jnp.sum(e, axis=-1, keepdims=True)
        ocat_ref[:, cols] = _dot(pr.astype(BF16), v).astype(BF16)
    x = x + _dot(ocat_ref[...], wo_ref[...])
    o_ref[...] = x
    if route:
        _route_rows(x, gf_ref, wr_ref, h_ref, meta_ref, rows_ref, cnt_ref)


def _xattn(x, g, wq, kv_l, wo, router=None):
    T = TOK_TILE
    tok = pl.BlockSpec((T, D_MODEL), lambda i: (i, 0))
    slab = pl.BlockSpec((T, LANES), lambda i: (i, 0))
    in_specs = [
        tok,
        _const_spec((1, D_MODEL)),
        _const_spec((D_MODEL, D_MODEL)),
        pl.BlockSpec((MEM_LEN, 2 * D_MODEL), lambda i: (i // (SEQ // T), 0)),
        _const_spec((D_MODEL, D_MODEL)),
    ]
    out_specs = [tok]
    out_shape = [jax.ShapeDtypeStruct((N_TOKENS, D_MODEL), F32)]
    args = (x, g, wq, kv_l, wo)
    if router is not None:
        in_specs += [_const_spec((1, D_MODEL)), _const_spec((LANES, D_MODEL))]
        out_specs += [tok, slab, pl.BlockSpec((None, N_EXPERTS, T), lambda i: (i, 0, 0)),
                      pl.BlockSpec((None, N_EXPERTS, LANES), lambda i: (i, 0, 0))]
        out_shape += [
            jax.ShapeDtypeStruct((N_TOKENS, D_MODEL), BF16),
            jax.ShapeDtypeStruct((N_TOKENS, LANES), jnp.int32),
            jax.ShapeDtypeStruct((N_BLOCKS, N_EXPERTS, T), F32),
            jax.ShapeDtypeStruct((N_BLOCKS, N_EXPERTS, LANES), jnp.int32),
        ]
        args += tuple(router)
    return pl.pallas_call(
        functools.partial(_xattn_kernel, route=router is not None),
        grid=(N_BLOCKS,),
        in_specs=in_specs,
        out_specs=out_specs,
        out_shape=out_shape,
        scratch_shapes=[pltpu.VMEM((T, D_MODEL), BF16)],
        compiler_params=_params(),
        name="xattn",
    )(*args)


def _swiglu_rows(h, wgu_ref, wd_ref, d_ff, n_chunks):
    fc = d_ff // n_chunks
    acc = None
    for c in range(n_chunks):
        g = _dot(h, wgu_ref[:, c * fc:(c + 1) * fc])
        up = _dot(h, wgu_ref[:, d_ff + c * fc:d_ff + (c + 1) * fc])
        a = (g * jax.nn.sigmoid(g) * up).astype(BF16)
        part = _dot(a, wd_ref[c * fc:(c + 1) * fc, :])
        acc = part if acc is None else acc + part
    return acc


def _ffn_kernel(x_ref, g_ref, wgu_ref, wd_ref, *rest):
    n_cast = (len(rest) - 1) // 2
    cast_in, o_ref, cast_out = rest[:n_cast], rest[n_cast], rest[n_cast + 1:]
    x = x_ref[...]
    h = _rms(x, g_ref[...]).astype(BF16)
    o_ref[...] = x + _swiglu_rows(h, wgu_ref, wd_ref, D_FF, FF_CHUNKS)
    for src, dst in zip(cast_in, cast_out):
        dst[...] = src[...].astype(BF16)


def _ffn(x, g, wgu, wd, cast_weights=()):
    T = TOK_TILE
    steps = N_TOKENS // T
    tok = pl.BlockSpec((T, D_MODEL), lambda i: (i, 0))
    cast_specs = [pl.BlockSpec((w.shape[0] // steps, w.shape[1]), lambda i: (i, 0))
                  for w in cast_weights]
    return pl.pallas_call(
        _ffn_kernel,
        grid=(steps,),
        in_specs=[tok, _const_spec((1, D_MODEL)), _const_spec((D_MODEL, 2 * D_FF)),
                  _const_spec((D_FF, D_MODEL))] + cast_specs,
        out_specs=[tok] + cast_specs,
        out_shape=[jax.ShapeDtypeStruct((N_TOKENS, D_MODEL), F32)]
                  + [jax.ShapeDtypeStruct(w.shape, BF16) for w in cast_weights],
        compiler_params=_params(),
        name="ffn",
    )(x, g, wgu, wd, *cast_weights)


def _route_rows(x, g_ref, wr_ref, h_ref, meta_ref, rows_ref, cnt_ref):
    T = TOK_TILE
    E = N_EXPERTS
    assert E == SUBLANES
    h = _rms(x, g_ref[...]).astype(BF16)
    h_ref[...] = h
    logits = lax.dot_general(wr_ref[...], h, (((1,), (1,)), ((), ())),
                             preferred_element_type=F32)[0:E]
    ex = lax.broadcasted_iota(jnp.int32, (E, T), 0)
    m1 = jnp.max(logits, axis=0, keepdims=True)
    i1 = jnp.min(jnp.where(logits == m1, ex, E), axis=0, keepdims=True)
    rest = jnp.where(ex == i1, -jnp.inf, logits)
    m2 = jnp.max(rest, axis=0, keepdims=True)
    i2 = jnp.min(jnp.where(rest == m2, ex, E), axis=0, keepdims=True)
    e2 = jnp.exp(m2 - m1)
    g1 = 1.0 / (1.0 + e2)
    g2 = e2 / (1.0 + e2)

    sel1 = ex == i1
    sel2 = ex == i2
    onehot = jnp.where(sel1, 1.0, jnp.where(sel2, 1.0, 0.0))
    earlier = (lax.broadcasted_iota(jnp.int32, (T, T), 0)
               < lax.broadcasted_iota(jnp.int32, (T, T), 1))
    onehot_mxu = jnp.concatenate([onehot, jnp.zeros((E, T), F32)], axis=0).astype(BF16)
    before = _dot(onehot_mxu, jnp.where(earlier, 1.0, 0.0).astype(BF16))[0:E]
    r1 = jnp.sum(jnp.where(sel1, before, 0.0), axis=0, keepdims=True)
    r2 = jnp.sum(jnp.where(sel2, before, 0.0), axis=0, keepdims=True)
    cnt_ref[...] = jnp.broadcast_to(jnp.sum(onehot, axis=1, keepdims=True), (E, LANES)).astype(jnp.int32)

    rows = jnp.where(ex == 0, i1.astype(F32), jnp.where(ex == 1, i2.astype(F32), jnp.where(
        ex == 2, r1, jnp.where(ex == 3, r2, jnp.where(ex == 4, g1, jnp.where(ex == 5, g2, 0.0))))))
    cols = jnp.concatenate([rows, jnp.zeros((LANES - E, T), F32)], axis=0).T
    meta_ref[...] = cols.astype(jnp.int32)
    rows_ref[...] = rows


def _segment_copies(src, src_row, dst, dst_row, n_rows, max_rows, sem, action):
    size = max_rows
    while size >= SUBLANES:
        done = pl.multiple_of((n_rows // (2 * size)) * (2 * size), SUBLANES)

        @pl.when((n_rows & size) != 0)
        def _(size=size, done=done):
            src_at = 0 if src_row is None else src_row + done
            copy = pltpu.make_async_copy(src.at[pl.ds(src_at, size)],
                                         dst.at[pl.ds(dst_row + done, size)], sem)
            copy.start() if action == "start" else copy.wait()

        size //= 2


def _block_positions_rows(rows_ref, lstart_s, b):
    pos = []
    for k in range(TOP_K):
        expert = rows_ref[k:k + 1, :].astype(jnp.int32)
        start = jnp.zeros_like(expert)
        for e in range(N_EXPERTS):
            start = jnp.where(expert == e, lstart_s[b * N_EXPERTS + e], start)
        pos.append(start + rows_ref[TOP_K + k:TOP_K + k + 1, :].astype(jnp.int32))
    return pos


def _block_positions(meta_ref, lstart_ref):
    lane = lax.broadcasted_iota(jnp.int32, (TOK_TILE, LANES), 1)
    meta = meta_ref[...]
    lstart = lstart_ref[...]
    pos = []
    for k in range(TOP_K):
        start = jnp.sum(jnp.where(lane == meta[:, k:k + 1], lstart, 0), axis=-1, keepdims=True)
        pos.append(start + meta[:, TOP_K + k:TOP_K + k + 1])
    return pos


def _dispatch_kernel(lstart_s, goff_s, pcnt_s, tail_off_s, tail_len_s, rows_ref, h_ref,
                     xs_ref, blk_ref, zero_ref, sem, fill_sem):
    b = pl.program_id(0)

    @pl.when(b == 0)
    def _():
        zero_ref[...] = jnp.zeros(zero_ref.shape, F32)
        fill_rows = zero_ref.shape[0]

        def unused_tiles(action):
            def body(i, carry):
                copy = pltpu.make_async_copy(
                    zero_ref, xs_ref.at[pl.ds(pl.multiple_of(i * fill_rows, SUBLANES), fill_rows)],
                    fill_sem)
                copy.start() if action == "start" else copy.wait()
                return carry
            lax.fori_loop(tail_off_s[N_EXPERTS] * (ROW_TILE // fill_rows),
                          N_ROW_TILES * (ROW_TILE // fill_rows), body, 0)

        for action in ("start", "wait"):
            for e in range(N_EXPERTS):
                _segment_copies(zero_ref, None, xs_ref, pl.multiple_of(tail_off_s[e], SUBLANES),
                                tail_len_s[e], fill_rows, fill_sem, action)
            unused_tiles(action)

    def segments(blk, slot, action):
        for e in range(N_EXPERTS):
            _segment_copies(blk_ref.at[slot], pl.multiple_of(lstart_s[blk * N_EXPERTS + e], SUBLANES),
                            xs_ref, pl.multiple_of(goff_s[blk * N_EXPERTS + e], SUBLANES),
                            pcnt_s[blk * N_EXPERTS + e], TOK_TILE, sem.at[slot], action)

    slot = b % 2
    p0, p1 = _block_positions_rows(rows_ref, lstart_s, b)
    r = lax.broadcasted_iota(jnp.int32, (BLOCK_ROWS, TOK_TILE), 0)
    sel = jnp.where(r == p0, 1.0, jnp.where(r == p1, 1.0, 0.0)).astype(BF16)
    blk_ref[slot] = _dot(sel, h_ref[...])
    segments(b, slot, "start")

    @pl.when(b > 0)
    def _():
        segments(b - 1, 1 - slot, "wait")

    @pl.when(b == N_BLOCKS - 1)
    def _():
        segments(b, slot, "wait")


def _dispatch(lstart, goff, pcnt, tail_off, tail_len, rows, h):
    T = TOK_TILE
    grid_spec = pltpu.PrefetchScalarGridSpec(
        num_scalar_prefetch=5,
        grid=(N_BLOCKS,),
        in_specs=[
            pl.BlockSpec((None, N_EXPERTS, T), lambda i, *_: (i, 0, 0)),
            pl.BlockSpec((T, D_MODEL), lambda i, *_: (i, 0)),
        ],
        out_specs=pl.BlockSpec(memory_space=pl.ANY),
        scratch_shapes=[pltpu.VMEM((2, BLOCK_ROWS, D_MODEL), F32),
                        pltpu.VMEM((ROW_TILE // 2, D_MODEL), F32),
                        pltpu.SemaphoreType.DMA((2,)),
                        pltpu.SemaphoreType.DMA(())],
    )
    return pl.pallas_call(
        _dispatch_kernel,
        grid_spec=grid_spec,
        out_shape=jax.ShapeDtypeStruct((N_SORTED_ROWS, D_MODEL), F32),
        compiler_params=_params(),
        name="dispatch",
    )(lstart, goff, pcnt, tail_off, tail_len, rows, h)


def _experts_kernel(tile_e_ref, tile_src_ref, xs_ref, wgu_ref, wd_ref, y_ref):
    del tile_e_ref
    on = tile_src_ref[pl.program_id(0)] == pl.program_id(0)

    @pl.when(on)
    def _():
        y_ref[...] = _swiglu_rows(xs_ref[...].astype(BF16), wgu_ref, wd_ref, D_FF_EXPERT,
                                  EXPERT_FF_CHUNKS)

    @pl.when(jnp.logical_not(on))
    def _():
        y_ref[...] = jnp.zeros((ROW_TILE, D_MODEL), F32)


def _experts(tile_e, tile_src, xs, wgu, wd):
    grid_spec = pltpu.PrefetchScalarGridSpec(
        num_scalar_prefetch=2,
        grid=(N_ROW_TILES,),
        in_specs=[
            pl.BlockSpec((ROW_TILE, D_MODEL), lambda i, te, src: (src[i], 0)),
            pl.BlockSpec((None, D_MODEL, 2 * D_FF_EXPERT), lambda i, te, src: (te[i], 0, 0),
                         pipeline_mode=pl.Buffered(1)),
            pl.BlockSpec((None, D_FF_EXPERT, D_MODEL), lambda i, te, src: (te[i], 0, 0),
                         pipeline_mode=pl.Buffered(1)),
        ],
        out_specs=pl.BlockSpec((ROW_TILE, D_MODEL), lambda i, te, src: (i, 0)),
    )
    return pl.pallas_call(
        _experts_kernel,
        grid_spec=grid_spec,
        out_shape=jax.ShapeDtypeStruct((N_SORTED_ROWS, D_MODEL), F32),
        compiler_params=_params(),
        name="experts",
    )(tile_e, tile_src, xs, wgu, wd)


def _combine_kernel(lstart_s, goff_s, pcnt_s, meta_ref, lstart_ref, rows_ref, x_ref, nf_ref, y_ref,
                    o_ref, buf_ref, sem):
    b = pl.program_id(0)
    slot = b % 2

    def fetch(blk, slot, action):
        for e in range(N_EXPERTS):
            _segment_copies(y_ref, pl.multiple_of(goff_s[blk * N_EXPERTS + e], SUBLANES),
                            buf_ref.at[slot], pl.multiple_of(lstart_s[blk * N_EXPERTS + e], SUBLANES),
                            pcnt_s[blk * N_EXPERTS + e], TOK_TILE, sem.at[slot], action)

    def prefetch(blk, slot):
        buf_ref[slot] = jnp.zeros(buf_ref.shape[1:], F32)
        fetch(blk, slot, "start")

    @pl.when(b == 0)
    def _():
        prefetch(0, 0)

    @pl.when(b + 1 < N_BLOCKS)
    def _():
        prefetch(b + 1, 1 - slot)

    fetch(b, slot, "wait")

    q0, q1 = _block_positions_rows(rows_ref, lstart_s, b)
    rr = lax.broadcasted_iota(jnp.int32, (BLOCK_ROWS, TOK_TILE), 0)
    row_gate = jnp.sum(jnp.where(rr == q0, rows_ref[2 * TOP_K:2 * TOP_K + 1, :],
                                 jnp.where(rr == q1, rows_ref[2 * TOP_K + 1:2 * TOP_K + 2, :], 0.0)),
                       axis=1, keepdims=True)
    y = (buf_ref[slot] * row_gate).astype(BF16)

    p0, p1 = _block_positions(meta_ref, lstart_ref)
    r = lax.broadcasted_iota(jnp.int32, (TOK_TILE, BLOCK_ROWS), 1)
    pick = jnp.where(r == p0, 1.0, jnp.where(r == p1, 1.0, 0.0)).astype(BF16)
    o_ref[...] = _rms(x_ref[...] + _dot(pick, y), nf_ref[...])


def _combine(lstart, goff, pcnt, meta, lstart_rows, rows, x, norm_final, y):
    T = TOK_TILE
    grid_spec = pltpu.PrefetchScalarGridSpec(
        num_scalar_prefetch=3,
        grid=(N_BLOCKS,),
        in_specs=[
            pl.BlockSpec((T, LANES), lambda i, *_: (i, 0)),
            pl.BlockSpec((None, 1, LANES), lambda i, *_: (i, 0, 0)),
            pl.BlockSpec((None, N_EXPERTS, T), lambda i, *_: (i, 0, 0)),
            pl.BlockSpec((T, D_MODEL), lambda i, *_: (i, 0)),
            pl.BlockSpec((1, D_MODEL), lambda i, *_: (0, 0)),
            pl.BlockSpec(memory_space=pl.ANY),
        ],
        out_specs=pl.BlockSpec((T, D_MODEL), lambda i, *_: (i, 0)),
        scratch_shapes=[pltpu.VMEM((2, BLOCK_ROWS, D_MODEL), F32), pltpu.SemaphoreType.DMA((2,))],
    )
    return pl.pallas_call(
        _combine_kernel,
        grid_spec=grid_spec,
        out_shape=jax.ShapeDtypeStruct((N_TOKENS, D_MODEL), F32),
        compiler_params=_params(),
        name="combine",
    )(lstart, goff, pcnt, meta, lstart_rows, rows, x, norm_final, y)


def _moe_layer(x, h, meta, rows, counts, w_gu, w_down, norm_final):
    i32 = lambda a: a.astype(jnp.int32)
    pcnt = -(-counts[:, :, 0] // SUBLANES) * SUBLANES
    lstart = jnp.cumsum(pcnt, axis=1) - pcnt
    rows_e = jnp.sum(pcnt, axis=0)
    tiles_e = -(-rows_e // ROW_TILE)
    tile_end = jnp.cumsum(tiles_e)
    region = (tile_end - tiles_e) * ROW_TILE
    goff = region[None, :] + jnp.cumsum(pcnt, axis=0) - pcnt
    tail_off = jnp.concatenate([region + rows_e, tile_end[-1:]])
    tail_len = tiles_e * ROW_TILE - rows_e
    tile_ids = jnp.arange(N_ROW_TILES, dtype=jnp.int32)
    tile_src = jnp.minimum(tile_ids, jnp.maximum(tile_end[-1] - 1, 0))
    tile_e = jnp.minimum(jnp.sum(tile_src[:, None] >= tile_end[None, :], axis=1), N_EXPERTS - 1)
    lstart_rows = jnp.pad(lstart, ((0, 0), (0, LANES - N_EXPERTS)))[:, None, :]
    flat = lambda a: i32(a).reshape(-1)

    xs = _dispatch(flat(lstart), flat(goff), flat(pcnt), i32(tail_off), i32(tail_len), rows, h)
    y = _experts(i32(tile_e), i32(tile_src), xs, w_gu, w_down)
    return _combine(flat(lstart), flat(goff), flat(pcnt), meta, i32(lstart_rows), rows, x,
                    norm_final, y)


def kernel(x, mem, norm_mem, norm_mix, w_in, pool_mix, pool_scale, sgu_ln_g, sgu_ln_b, w_spatial, b_spatial, w_branch_a, w_branch_b, w_out, norm_xattn, w_xq, w_xkv, w_xo, norm_ffn, w_ff_gu, w_ff_down, w_router, w_moe_gu, w_moe_down, norm_final):
    assert DEPTH == 2 and x.shape == (BATCH, SEQ, D_MODEL)
    row = lambda a: a.reshape(1, -1)
    bf = lambda a: a.astype(BF16)

    xt = x.reshape(N_TOKENS, D_MODEL)
    kv = _kv(mem.reshape(BATCH * MEM_LEN, D_MODEL), row(norm_mem), bf(w_xkv))
    for l in range(DEPTH):
        bsp_tile = jnp.repeat(b_spatial[l].T, SGU_GROUP_DIM, axis=1)
        xt = _mixer(xt, row(norm_mix[l]), bf(w_in[l]), bf(pool_mix[l]), row(pool_scale[l]),
                    row(sgu_ln_g[l]), row(sgu_ln_b[l]), w_spatial[l], bsp_tile,
                    bf(w_branch_a[l]), bf(w_branch_b[l]), bf(w_out[l]))
        attn = (xt, row(norm_xattn[l]), bf(w_xq[l]), kv[l], bf(w_xo[l]))
        if l % 2 == 0:
            (xt,) = _xattn(*attn)
            m = (l + 1) // 2
            xt, moe_gu, moe_down = _ffn(
                xt, row(norm_ffn[l]), bf(w_ff_gu[l // 2]), bf(w_ff_down[l // 2]),
                cast_weights=(w_moe_gu[m].reshape(N_EXPERTS * D_MODEL, 2 * D_FF_EXPERT),
                              w_moe_down[m].reshape(N_EXPERTS * D_FF_EXPERT, D_MODEL)))
        else:
            wr_pad = bf(jnp.pad(w_router[l // 2].T, ((0, LANES - N_EXPERTS), (0, 0))))
            xt, h, meta, rows, counts = _xattn(*attn, router=(row(norm_ffn[l]), wr_pad))
            xt = _moe_layer(xt, h, meta, rows, counts,
                            moe_gu.reshape(N_EXPERTS, D_MODEL, 2 * D_FF_EXPERT),
                            moe_down.reshape(N_EXPERTS, D_FF_EXPERT, D_MODEL), row(norm_final))
    return xt.reshape(BATCH, SEQ, D_MODEL)
```
